```python
import jax, jax.numpy as jnp
from jax import lax
import numpy as np

D_MODEL = 1024
BATCH = 32
SEQ = 2048
DEPTH = 2
DEC_BATCH = 16
DEC_SEQ = 4096
PAST_LEN = 128

N_MIXERS = 2
CONV_WIDTH = 3
DILATED_CONFIGS = ((128, 1), (512, 4), (2048, 16))
N_GROUPS = len(DILATED_CONFIGS)
HEADS_PER_GROUP = 16
HEAD_DIM = D_MODEL // HEADS_PER_GROUP
N_ATTN_HEADS = N_GROUPS * HEADS_PER_GROUP
ATTN_OUT_WIDTH = HEADS_PER_GROUP * HEAD_DIM
PEER_HEADS = 8
PEER_KEY_DIM = 256
PEER_HALF = PEER_KEY_DIM // 2
N_KEYS = 128
N_EXPERTS = N_KEYS * N_KEYS
PEER_TOPK = 16
TOK_BLOCK = 128
RMS_EPS = 1e-6

kernel_name = 'hybrid_conv_dilated_alibi_peer_encoder'


def rmsnorm(x, g):
    xf = x.astype(jnp.float32)
    y = xf * lax.rsqrt(jnp.mean(xf * xf, axis=-1, keepdims=True) + RMS_EPS)
    return (y * g.astype(jnp.float32)).astype(x.dtype)


def alibi_slopes(n):
    return jnp.exp2(-8.0 * jnp.arange(1, n + 1, dtype=jnp.float32) / n)


def short_conv_mixer(h, w_in, conv_w, w_out):
    b, c, u = jnp.split(h @ w_in, 3, axis=-1)
    z = jnp.pad(c * u, ((0, 0), (1, 1), (0, 0)))
    conv = z[:, :-2] * conv_w[0] + z[:, 1:-1] * conv_w[1] + z[:, 2:] * conv_w[2]
    return (b * conv) @ w_out


def banded_dilated_attention(q, k, v, dilation, w_side, slopes):
    S, H, dh = q.shape
    L = S // dilation
    blk = w_side
    nb = -(-L // blk)
    Lp = nb * blk

    def strided(a):
        return a.reshape(L, dilation, H, dh).transpose(1, 0, 2, 3)

    qb = jnp.pad(strided(q), ((0, 0), (0, Lp - L), (0, 0), (0, 0))).reshape(dilation, nb, blk, H, dh)

    def band(a):
        ap = jnp.pad(strided(a), ((0, 0), (blk, Lp - L + blk), (0, 0), (0, 0)))
        return jnp.concatenate(
            [ap[:, j * blk:j * blk + Lp].reshape(dilation, nb, blk, H, dh) for j in range(3)], axis=2)

    kb = band(k)
    vb = band(v)
    a_idx = jnp.arange(blk)
    c_idx = jnp.arange(3 * blk)
    rel = c_idx[None, :] - blk - a_idx[:, None]
    key_pos = jnp.arange(nb)[:, None] * blk - blk + c_idx[None, :]
    valid = (jnp.abs(rel) <= w_side)[None] & ((key_pos >= 0) & (key_pos < L))[:, None, :]
    dist = (dilation * jnp.abs(rel)).astype(jnp.float32)
    bias = -slopes[:, None, None] * dist[None]
    s = jnp.einsum('rnqhd,rnkhd->rnhqk', qb, kb).astype(jnp.float32) * (HEAD_DIM ** -0.5) + bias[None, None]
    s = jnp.where(valid[None, :, None], s, -jnp.inf)
    m = jnp.max(s, axis=-1, keepdims=True)
    p = jnp.exp(s - m)
    den = jnp.sum(p, axis=-1)
    o = jnp.einsum('rnhqk,rnkhd->rnqhd', p, vb.astype(jnp.float32)) / den.transpose(0, 1, 3, 2)[..., None]
    lse = (m[..., 0] + jnp.log(den)).transpose(0, 1, 3, 2)
    o = o.reshape(dilation, Lp, H, dh)[:, :L].transpose(1, 0, 2, 3).reshape(S, H, dh)
    lse = lse.reshape(dilation, Lp, H)[:, :L].transpose(1, 0, 2).reshape(S, H)
    return o, lse


def dilated_attention_mixer(h, w_qkv, w_o):
    slopes = alibi_slopes(N_ATTN_HEADS).reshape(N_GROUPS, HEADS_PER_GROUP)

    def per_seq(h_s):
        S = h_s.shape[0]
        qkv = (h_s @ w_qkv).reshape(S, N_GROUPS, 3, HEADS_PER_GROUP, HEAD_DIM)
        outs, lses = [], []
        for g, (window, dil) in enumerate(DILATED_CONFIGS):
            o, l = banded_dilated_attention(qkv[:, g, 0], qkv[:, g, 1], qkv[:, g, 2],
                                            dil, window // (2 * dil), slopes[g])
            outs.append(o)
            lses.append(l)
        o = jnp.stack(outs, axis=1)
        alpha = jax.nn.softmax(jnp.stack(lses, axis=1), axis=1)
        merged = jnp.einsum('sghd,sgh->shd', o, alpha).reshape(S, ATTN_OUT_WIDTH).astype(h_s.dtype)
        return merged @ w_o

    return lax.map(per_seq, h)


def peer_ffn(h, w_query, sub_keys, expert_u, expert_v):
    B, S, D = h.shape
    xt = h.reshape((B * S) // TOK_BLOCK, TOK_BLOCK, D)

    def block(xb):
        t = xb.shape[0]
        q = (xb @ w_query).reshape(t, PEER_HEADS, 2, PEER_HALF)
        s = jnp.einsum('thpd,hpkd->thpk', q, sub_keys).astype(jnp.float32)
        s1, i1 = lax.top_k(s[:, :, 0], PEER_TOPK)
        s2, i2 = lax.top_k(s[:, :, 1], PEER_TOPK)
        cand = (s1[..., :, None] + s2[..., None, :]).reshape(t, PEER_HEADS, PEER_TOPK * PEER_TOPK)
        cidx = (i1[..., :, None] * N_KEYS + i2[..., None, :]).reshape(t, PEER_HEADS, PEER_TOPK * PEER_TOPK)
        top_s, pos = lax.top_k(cand, PEER_TOPK)
        idx = jnp.take_along_axis(cidx, pos, axis=-1)
        gate = jax.nn.softmax(top_s, axis=-1)
        u = expert_u[idx]
        act = jax.nn.gelu(jnp.einsum('thkd,td->thk', u, xb), approximate=False)
        w = (gate * act.astype(jnp.float32)).astype(xb.dtype)
        return jnp.einsum('thk,thkd->td', w, expert_v[idx])

    return lax.map(block, xt).reshape(B, S, D)


def setup_inputs(seed: int = 0) -> dict:
    key = jax.random.key(seed)
    ks = jax.random.split(key, 24)
    f32 = jnp.float32
    D = D_MODEL

    def nrm(k, shape, scale):
        return jax.random.normal(k, shape, f32) * scale

    def gain(k):
        return 1.0 + 0.01 * jax.random.normal(k, (D,), f32)

    return {
        'x_prompt': nrm(ks[0], (BATCH, SEQ, D), 1.0),
        'x_sample': nrm(ks[1], (DEC_BATCH, DEC_SEQ, D), 1.0),
        'l0_norm_mix': gain(ks[2]),
        'l0_conv_w_in': nrm(ks[3], (D, 3 * D), D ** -0.5),
        'l0_conv_w': nrm(ks[4], (CONV_WIDTH, D), CONV_WIDTH ** -0.5),
        'l0_conv_w_out': nrm(ks[5], (D, D), D ** -0.5),
        'l0_norm_ffn': gain(ks[6]),
        'l0_peer_w_query': nrm(ks[7], (D, PEER_HEADS * PEER_KEY_DIM), D ** -0.5),
        'l0_peer_sub_keys': nrm(ks[8], (PEER_HEADS, 2, N_KEYS, PEER_HALF), PEER_HALF ** -0.5),
        'l0_peer_u': nrm(ks[9], (N_EXPERTS, D), D ** -0.5),
        'l0_peer_v': nrm(ks[10], (N_EXPERTS, D), PEER_HEADS ** -0.5),
        'l1_norm_mix': gain(ks[11]),
        'l1_attn_w_qkv': nrm(ks[12], (D, N_GROUPS * 3 * HEADS_PER_GROUP * HEAD_DIM), D ** -0.5),
        'l1_attn_w_o': nrm(ks[13], (ATTN_OUT_WIDTH, D), ATTN_OUT_WIDTH ** -0.5),
        'l1_norm_ffn': gain(ks[14]),
        'l1_peer_w_query': nrm(ks[15], (D, PEER_HEADS * PEER_KEY_DIM), D ** -0.5),
        'l1_peer_sub_keys': nrm(ks[16], (PEER_HEADS, 2, N_KEYS, PEER_HALF), PEER_HALF ** -0.5),
        'l1_peer_u': nrm(ks[17], (N_EXPERTS, D), D ** -0.5),
        'l1_peer_v': nrm(ks[18], (N_EXPERTS, D), PEER_HEADS ** -0.5),
        'norm_final': gain(ks[19]),
    }


def reference(x_prompt, x_sample,
              l0_norm_mix, l0_conv_w_in, l0_conv_w, l0_conv_w_out,
              l0_norm_ffn, l0_peer_w_query, l0_peer_sub_keys, l0_peer_u, l0_peer_v,
              l1_norm_mix, l1_attn_w_qkv, l1_attn_w_o,
              l1_norm_ffn, l1_peer_w_query, l1_peer_sub_keys, l1_peer_u, l1_peer_v,
              norm_final):
    mixer_fns = (short_conv_mixer, dilated_attention_mixer)
    mixer_params = ((l0_conv_w_in, l0_conv_w, l0_conv_w_out),
                    (l1_attn_w_qkv, l1_attn_w_o))
    layer_params = ((l0_norm_mix, l0_norm_ffn, (l0_peer_w_query, l0_peer_sub_keys, l0_peer_u, l0_peer_v)),
                    (l1_norm_mix, l1_norm_ffn, (l1_peer_w_query, l1_peer_sub_keys, l1_peer_u, l1_peer_v)))

    def trunk(x):
        for i in range(DEPTH):
            norm_mix, norm_ffn, peer_p = layer_params[i]
            x = x + mixer_fns[i % N_MIXERS](rmsnorm(x, norm_mix), *mixer_params[i])
            x = x + peer_ffn(rmsnorm(x, norm_ffn), *peer_p)
        return rmsnorm(x, norm_final)

    y_prompt = trunk(x_prompt)
    y_sample = trunk(x_sample)
    return (y_prompt, y_sample)
```

```python
import functools

import jax
import jax.numpy as jnp
from jax import lax
from jax.experimental import pallas as pl
from jax.experimental.pallas import tpu as pltpu

D_MODEL = 1024
RMS_EPS = 1e-6
TOKEN_TILE = 512
V7X_VMEM_LIMIT_BYTES = 56 * 1024 * 1024


def _params(n_grid_dims, vmem=None):
    return pltpu.CompilerParams(
        dimension_semantics=("arbitrary",) * n_grid_dims,
        vmem_limit_bytes=vmem)


def _rms(x, g):
    return x * lax.rsqrt(jnp.mean(x * x, axis=-1, keepdims=True) + RMS_EPS) * g


def _conv_in_kernel(x_ref, g_ref, w_ref, b_ref, z_ref):
    h = _rms(x_ref[...], g_ref[...]).astype(jnp.bfloat16)
    y = jnp.dot(h, w_ref[...], preferred_element_type=jnp.float32)
    d = D_MODEL
    b_ref[...] = y[:, :d]
    z_ref[...] = y[:, d:2 * d] * y[:, 2 * d:]


def _conv_in(x, g, w_in_bf16):
    n, d = x.shape
    tm = TOKEN_TILE
    return pl.pallas_call(
        _conv_in_kernel,
        grid=(n // tm,),
        in_specs=[
            pl.BlockSpec((tm, d), lambda i: (i, 0)),
            pl.BlockSpec((1, d), lambda i: (0, 0)),
            pl.BlockSpec((d, 3 * d), lambda i: (0, 0)),
        ],
        out_specs=[pl.BlockSpec((tm, d), lambda i: (i, 0)),
                   pl.BlockSpec((tm, d), lambda i: (i, 0))],
        out_shape=[jax.ShapeDtypeStruct((n, d), jnp.float32)] * 2,
        compiler_params=_params(1, V7X_VMEM_LIMIT_BYTES),
        name="conv_in",
    )(x, g.reshape(1, d), w_in_bf16)


def _conv_out_kernel(tiles_per_seq, x_ref, b_ref, z_ref, zp_ref, zn_ref, cw_ref, w_ref, o_ref):
    i = pl.program_id(0)
    pos = i % tiles_per_seq
    z = z_ref[...]
    tm = z.shape[0]
    row = lax.broadcasted_iota(jnp.int32, z.shape, 0)
    prev_row = jnp.where(pos == 0, 0.0, zp_ref[7:8, :])
    next_row = jnp.where(pos == tiles_per_seq - 1, 0.0, zn_ref[0:1, :])
    z_prev = jnp.where(row == 0, prev_row, pltpu.roll(z, 1, 0))
    z_next = jnp.where(row == tm - 1, next_row, pltpu.roll(z, tm - 1, 0))
    cw = cw_ref[...]
    conv = z_prev * cw[0:1, :] + z * cw[1:2, :] + z_next * cw[2:3, :]
    a = (b_ref[...] * conv).astype(jnp.bfloat16)
    o_ref[...] = x_ref[...] + jnp.dot(a, w_ref[...], preferred_element_type=jnp.float32)


def _conv_out(x, b, z, conv_w, w_out_bf16, seq_len):
    n, d = x.shape
    tm = TOKEN_TILE
    tps = seq_len // tm
    sub = tm // 8
    nb8 = n // 8
    tile = lambda i: (i, 0)
    return pl.pallas_call(
        functools.partial(_conv_out_kernel, tps),
        grid=(n // tm,),
        in_specs=[
            pl.BlockSpec((tm, d), tile),
            pl.BlockSpec((tm, d), tile),
            pl.BlockSpec((tm, d), tile),
            pl.BlockSpec((8, d), lambda i: (jnp.maximum(i * sub - 1, 0), 0)),
            pl.BlockSpec((8, d), lambda i: (jnp.minimum((i + 1) * sub, nb8 - 1), 0)),
            pl.BlockSpec((3, d), lambda i: (0, 0)),
            pl.BlockSpec((d, d), lambda i: (0, 0)),
        ],
        out_specs=pl.BlockSpec((tm, d), tile),
        out_shape=jax.ShapeDtypeStruct((n, d), jnp.float32),
        compiler_params=_params(1, V7X_VMEM_LIMIT_BYTES),
        name="conv_out",
    )(x, b, z, z, z, conv_w, w_out_bf16)


def conv_mixer_layer(x, g, w_in, conv_w, w_out, seq_len):
    b, z = _conv_in(x, g, w_in.astype(jnp.bfloat16))
    return _conv_out(x, b, z, conv_w, w_out.astype(jnp.bfloat16), seq_len)


DILATED_CONFIGS = ((128, 1), (512, 4), (2048, 16))
N_GROUPS = len(DILATED_CONFIGS)
HEADS_PER_GROUP = 16
HEAD_DIM = D_MODEL // HEADS_PER_GROUP
ATTN_SIDE = 64
ATTN_Q_BLOCK = 128
LSE_LANES = 128


def _qkv_kernel(x_ref, g_ref, w_ref, o_ref):
    h = _rms(x_ref[...], g_ref[...]).astype(jnp.bfloat16)
    d = D_MODEL
    for c in range(o_ref.shape[1] // d):
        o_ref[:, c * d:(c + 1) * d] = jnp.dot(
            h, w_ref[:, c * d:(c + 1) * d], preferred_element_type=jnp.float32).astype(jnp.bfloat16)


def _qkv_proj(x, g, w_bf16):
    n, d = x.shape
    tm = TOKEN_TILE
    c = w_bf16.shape[1]
    return pl.pallas_call(
        _qkv_kernel,
        grid=(n // tm,),
        in_specs=[
            pl.BlockSpec((tm, d), lambda i: (i, 0)),
            pl.BlockSpec((1, d), lambda i: (0, 0)),
            pl.BlockSpec((d, c), lambda i: (0, 0), pipeline_mode=pl.Buffered(1)),
        ],
        out_specs=pl.BlockSpec((tm, c), lambda i: (i, 0)),
        out_shape=jax.ShapeDtypeStruct((n, c), jnp.bfloat16),
        compiler_params=_params(1, V7X_VMEM_LIMIT_BYTES),
        name="qkv_proj",
    )(x, g.reshape(1, d), w_bf16)


def _alibi_slope(head):
    n = N_GROUPS * HEADS_PER_GROUP
    return 2.0 ** (-8.0 * (head + 1) / n)


def _attn_kernel(group, dilation, seq_strided, q_ref, kp_ref, kc_ref, kn_ref,
                 vp_ref, vc_ref, vn_ref, o_ref, lse_ref):
    j = pl.program_id(2)
    qb = ATTN_Q_BLOCK
    nk = qb + 2 * ATTN_SIDE
    qi = lax.broadcasted_iota(jnp.int32, (qb, nk), 0)
    ci = lax.broadcasted_iota(jnp.int32, (qb, nk), 1)
    rel = ci - ATTN_SIDE - qi
    key_pos = j * qb - ATTN_SIDE + ci
    valid = (jnp.abs(rel) <= ATTN_SIDE) & (key_pos >= 0) & (key_pos < seq_strided)
    dist = (dilation * jnp.abs(rel)).astype(jnp.float32)
    lane = lax.broadcasted_iota(jnp.int32, (qb, LSE_LANES), 1)
    lse_all = jnp.zeros((qb, LSE_LANES), jnp.float32)
    for h in range(HEADS_PER_GROUP):
        cols = slice(h * HEAD_DIM, (h + 1) * HEAD_DIM)
        q = q_ref[0, :, cols]
        k = jnp.concatenate([kp_ref[0, :, cols], kc_ref[0, :, cols], kn_ref[0, :, cols]], axis=0)
        v = jnp.concatenate([vp_ref[0, :, cols], vc_ref[0, :, cols], vn_ref[0, :, cols]], axis=0)
        s = lax.dot_general(q, k, (((1,), (1,)), ((), ())), preferred_element_type=jnp.float32)
        slope = _alibi_slope(group * HEADS_PER_GROUP + h)
        s = s * (HEAD_DIM ** -0.5) - slope * dist
        s = jnp.where(valid, s, -jnp.inf)
        m = jnp.max(s, axis=-1, keepdims=True)
        p = jnp.exp(s - m)
        den = jnp.sum(p, axis=-1, keepdims=True)
        o = jnp.dot(p.astype(jnp.bfloat16), v, preferred_element_type=jnp.float32)
        o_ref[0, :, cols] = o / den
        lse_all = jnp.where(lane == h, m + jnp.log(den), lse_all)
    lse_ref[0] = lse_all


def _group_attention(qkv, group, dilation, batch, seq_len):
    d = D_MODEL
    ls = seq_len // dilation
    width = qkv.shape[1] // d
    qb = ATTN_Q_BLOCK
    sb = qb // ATTN_SIDE
    n_side = ls // ATTN_SIDE
    view = qkv.reshape(batch, ls, dilation * width * d)
    col0 = group * 3

    def cur(c):
        return pl.BlockSpec((1, qb, d), lambda b, r, j: (b, j, r * width + col0 + c))

    def prev(c):
        return pl.BlockSpec((1, ATTN_SIDE, d),
                            lambda b, r, j: (b, jnp.maximum(j * sb - 1, 0), r * width + col0 + c))

    def nxt(c):
        return pl.BlockSpec((1, ATTN_SIDE, d),
                            lambda b, r, j: (b, jnp.minimum((j + 1) * sb, n_side - 1), r * width + col0 + c))

    o, lse = pl.pallas_call(
        functools.partial(_attn_kernel, group, dilation, ls),
        grid=(batch, dilation, ls // qb),
        in_specs=[cur(0), prev(1), cur(1), nxt(1), prev(2), cur(2), nxt(2)],
        out_specs=[pl.BlockSpec((1, qb, d), lambda b, r, j: (b, j, r)),
                   pl.BlockSpec((1, qb, LSE_LANES), lambda b, r, j: (b, j, r))],
        out_shape=[jax.ShapeDtypeStruct((batch, ls, dilation * d), jnp.float32),
                   jax.ShapeDtypeStruct((batch, ls, dilation * LSE_LANES), jnp.float32)],
        compiler_params=_params(3, V7X_VMEM_LIMIT_BYTES),
        name=f"attn_g{group}",
    )(view, view, view, view, view, view, view)
    n = batch * seq_len
    return o.reshape(n, d), lse.reshape(n, LSE_LANES)


def _merge_kernel(x_ref, o0_ref, o1_ref, o2_ref, l0_ref, l1_ref, l2_ref, e_ref, w_ref, out_ref):
    l0, l1, l2 = l0_ref[...], l1_ref[...], l2_ref[...]
    m = jnp.maximum(jnp.maximum(l0, l1), l2)
    e0, e1, e2 = jnp.exp(l0 - m), jnp.exp(l1 - m), jnp.exp(l2 - m)
    tot = e0 + e1 + e2
    expand = lambda a: jnp.dot(a, e_ref[...], preferred_element_type=jnp.float32,
                               precision=lax.Precision.HIGHEST)
    merged = (expand(e0 / tot) * o0_ref[...] + expand(e1 / tot) * o1_ref[...]
              + expand(e2 / tot) * o2_ref[...])
    out_ref[...] = x_ref[...] + jnp.dot(merged.astype(jnp.bfloat16), w_ref[...],
                                        preferred_element_type=jnp.float32)


def _attn_merge(x, outs, lses, w_o_bf16):
    n, d = x.shape
    tm = TOKEN_TILE
    head_of_col = jnp.arange(d, dtype=jnp.int32) // HEAD_DIM
    expand = (jnp.arange(LSE_LANES, dtype=jnp.int32)[:, None] == head_of_col[None, :]).astype(jnp.float32)
    tile = lambda i: (i, 0)
    const = lambda i: (0, 0)
    return pl.pallas_call(
        _merge_kernel,
        grid=(n // tm,),
        in_specs=[pl.BlockSpec((tm, d), tile)] * 4 + [pl.BlockSpec((tm, LSE_LANES), tile)] * 3
        + [pl.BlockSpec((LSE_LANES, d), const), pl.BlockSpec((d, d), const)],
        out_specs=pl.BlockSpec((tm, d), tile),
        out_shape=jax.ShapeDtypeStruct((n, d), jnp.float32),
        compiler_params=_params(1, V7X_VMEM_LIMIT_BYTES),
        name="attn_merge",
    )(x, *outs, *lses, expand, w_o_bf16)


def attn_mixer_layer(x, g, w_qkv, w_o, seq_len):
    n = x.shape[0]
    batch = n // seq_len
    qkv = _qkv_proj(x, g, w_qkv.astype(jnp.bfloat16))
    outs, lses = [], []
    for group, (window, dilation) in enumerate(DILATED_CONFIGS):
        assert window // (2 * dilation) == ATTN_SIDE
        o, lse = _group_attention(qkv, group, dilation, batch, seq_len)
        outs.append(o)
        lses.append(lse)
    return _attn_merge(x, outs, lses, w_o.astype(jnp.bfloat16))


PEER_HEADS = 8
PEER_HALF = 128
N_KEYS = 128
PEER_TOPK = 16
PEER_TILE = 128
SLOTS = PEER_HEADS * PEER_TOPK
LANES = 128
HALF_D = D_MODEL // 2
ROWS_PER_EXPERT = HALF_D // LANES
SUBLANES = 8


def _take_top(s, ids, n, payload=None):
    vals, picks = [], []
    for _ in range(n):
        m = jnp.max(s, axis=0, keepdims=True)
        sel = jnp.min(jnp.where(s == m, ids, jnp.int32(2 ** 30)), axis=0, keepdims=True)
        hit = ids == sel
        if payload is None:
            picks.append(sel)
        else:
            picks.append(jnp.max(jnp.where(hit, payload, -1), axis=0, keepdims=True))
        s = jnp.where(hit, -jnp.inf, s)
        vals.append(m)
    return vals, picks


def _peer_select_kernel(x_ref, g_ref, wq_ref, keys_ref, h_ref, idx_ref, gate_ref, q_scr):
    hd = pl.program_id(1)
    t = x_ref.shape[0]

    @pl.when(hd == 0)
    def _():
        h = _rms(x_ref[...], g_ref[...])
        h_ref[...] = h
        q = jnp.dot(h.astype(jnp.bfloat16), wq_ref[...], preferred_element_type=jnp.float32)
        for k in range(PEER_HEADS):
            q_scr[k] = q[:, k * 2 * PEER_HALF:(k + 1) * 2 * PEER_HALF].astype(jnp.bfloat16)

    q = q_scr[hd]
    key_id = lax.broadcasted_iota(jnp.int32, (N_KEYS, t), 0)
    halves = []
    for p in range(2):
        s = lax.dot_general(keys_ref[0, p], q[:, p * PEER_HALF:(p + 1) * PEER_HALF],
                            (((1,), (1,)), ((), ())), preferred_element_type=jnp.float32)
        vals, ids = _take_top(s, key_id, PEER_TOPK)
        halves.append((jnp.concatenate(vals, axis=0), jnp.concatenate(ids, axis=0)))
    (s1, i1), (s2, i2) = halves

    cv, cp, ci = [], [], []
    for b in range(SUBLANES):
        rows = PEER_TOPK if b == 0 else SUBLANES
        a_id = lax.broadcasted_iota(jnp.int32, (rows, t), 0)
        v = s1[0:rows] + s2[b:b + 1]
        na = PEER_TOPK // (b + 1)
        if na < rows:
            v = jnp.where(a_id < na, v, -jnp.inf)
        cv.append(v)
        cp.append(a_id * PEER_TOPK + b)
        ci.append(i1[0:rows] * N_KEYS + i2[b:b + 1])
    b_id = lax.broadcasted_iota(jnp.int32, (SUBLANES, t), 0) + SUBLANES
    cv.append(s1[0:1] + s2[SUBLANES:])
    cp.append(b_id)
    ci.append(i1[0:1] * N_KEYS + i2[SUBLANES:])
    top, experts = _take_top(jnp.concatenate(cv, axis=0), jnp.concatenate(cp, axis=0), PEER_TOPK,
                             payload=jnp.concatenate(ci, axis=0))
    top = jnp.concatenate(top, axis=0)
    e = jnp.exp(top - top[0:1])
    gate_ref[0] = e / jnp.sum(e, axis=0, keepdims=True)
    idx_ref[0] = jnp.concatenate(experts, axis=0) * ROWS_PER_EXPERT


def _peer_select(x, g, wq_bf16, keys_bf16):
    n, d = x.shape
    t = PEER_TILE
    nt = n // t
    return pl.pallas_call(
        _peer_select_kernel,
        grid=(nt, PEER_HEADS),
        in_specs=[
            pl.BlockSpec((t, d), lambda i, h: (i, 0)),
            pl.BlockSpec((1, d), lambda i, h: (0, 0)),
            pl.BlockSpec(wq_bf16.shape, lambda i, h: (0, 0)),
            pl.BlockSpec((1, 2, N_KEYS, PEER_HALF), lambda i, h: (h, 0, 0, 0)),
        ],
        out_specs=[pl.BlockSpec((t, d), lambda i, h: (i, 0)),
                   pl.BlockSpec((1, PEER_TOPK, t), lambda i, h: (i, h, 0)),
                   pl.BlockSpec((1, PEER_TOPK, t), lambda i, h: (i, h, 0))],
        out_shape=[jax.ShapeDtypeStruct((n, d), jnp.float32),
                   jax.ShapeDtypeStruct((nt, SLOTS, t), jnp.int32),
                   jax.ShapeDtypeStruct((nt, SLOTS, t), jnp.float32)],
        scratch_shapes=[pltpu.VMEM((PEER_HEADS, t, 2 * PEER_HALF), jnp.bfloat16)],
        compiler_params=_params(2, V7X_VMEM_LIMIT_BYTES),
        name="peer_select",
    )(x, g.reshape(1, d), wq_bf16, keys_bf16)


def _pack_table(tab):
    bits = lax.bitcast_convert_type(tab.astype(jnp.bfloat16), jnp.uint16).astype(jnp.uint32)
    packed = bits[:, :HALF_D] | (bits[:, HALF_D:] << 16)
    return packed.reshape(tab.shape[0] * ROWS_PER_EXPERT, LANES)


def _unpack_row(row):
    lo = pltpu.bitcast(row << 16, jnp.float32)
    hi = pltpu.bitcast(row & jnp.uint32(0xFFFF0000), jnp.float32)
    return lo, hi


def _peer_act_kernel(idx_ref, h_ref, tab_ref, gate_ref, w_ref, prod_scr):
    t_tile = PEER_TILE
    lane = lax.broadcasted_iota(jnp.int32, (SLOTS, t_tile), 1)

    def token(t, act):
        hv = h_ref[pl.ds(pl.multiple_of(t * SUBLANES, SUBLANES), SUBLANES), :]
        h_lo, h_hi = hv[0:ROWS_PER_EXPERT], hv[ROWS_PER_EXPERT:]
        for s in range(SLOTS):
            r = pl.multiple_of(idx_ref[s * t_tile + t], ROWS_PER_EXPERT)
            lo, hi = _unpack_row(tab_ref[pl.ds(r, ROWS_PER_EXPERT), :])
            prod_scr[s * ROWS_PER_EXPERT:(s + 1) * ROWS_PER_EXPERT, :] = lo * h_lo + hi * h_hi
        part = prod_scr[pl.ds(0, SLOTS, stride=ROWS_PER_EXPERT), :]
        for k in range(1, ROWS_PER_EXPERT):
            part = part + prod_scr[pl.ds(k, SLOTS, stride=ROWS_PER_EXPERT), :]
        col = jnp.sum(part, axis=1, keepdims=True)
        return jnp.where(lane == t, col, act)

    act = lax.fori_loop(0, t_tile, token, jnp.zeros((SLOTS, t_tile), jnp.float32))
    gelu = 0.5 * act * (1.0 + lax.erf(act * (0.5 ** 0.5)))
    w_ref[0] = gate_ref[0] * gelu


def _peer_act(idx_flat, h_rows, u_packed, gate):
    nt = gate.shape[0]
    t = PEER_TILE
    return pl.pallas_call(
        _peer_act_kernel,
        grid=(nt,),
        in_specs=[
            pl.BlockSpec((SLOTS * t,), lambda i: (i,), memory_space=pltpu.SMEM),
            pl.BlockSpec((t * SUBLANES, LANES), lambda i: (i, 0)),
            pl.BlockSpec(u_packed.shape, lambda i: (0, 0), pipeline_mode=pl.Buffered(1)),
            pl.BlockSpec((1, SLOTS, t), lambda i: (i, 0, 0)),
        ],
        out_specs=pl.BlockSpec((1, SLOTS, t), lambda i: (i, 0, 0)),
        out_shape=jax.ShapeDtypeStruct((nt, SLOTS, t), jnp.float32),
        scratch_shapes=[pltpu.VMEM((SLOTS * ROWS_PER_EXPERT, LANES), jnp.float32)],
        compiler_params=_params(1, V7X_VMEM_LIMIT_BYTES),
        name="peer_act",
    )(idx_flat, h_rows, u_packed, gate)


def _peer_out_kernel(idx_ref, w_ref, x_ref, tab_ref, o_ref):
    t_tile = PEER_TILE

    def token(t, carry):
        acc_lo = jnp.zeros((ROWS_PER_EXPERT, LANES), jnp.float32)
        acc_hi = jnp.zeros((ROWS_PER_EXPERT, LANES), jnp.float32)
        for s in range(SLOTS):
            r = pl.multiple_of(idx_ref[s * t_tile + t], ROWS_PER_EXPERT)
            lo, hi = _unpack_row(tab_ref[pl.ds(r, ROWS_PER_EXPERT), :])
            w = w_ref[s * t_tile + t]
            acc_lo = acc_lo + w * lo
            acc_hi = acc_hi + w * hi
        rows = pl.ds(pl.multiple_of(t * SUBLANES, SUBLANES), SUBLANES)
        o_ref[rows, :] = x_ref[rows, :] + jnp.concatenate([acc_lo, acc_hi], axis=0)
        return carry

    lax.fori_loop(0, t_tile, token, 0)


def _peer_out(idx_flat, w_flat, x_rows, v_packed):
    t = PEER_TILE
    nt = idx_flat.shape[0] // (SLOTS * t)
    return pl.pallas_call(
        _peer_out_kernel,
        grid=(nt,),
        in_specs=[
            pl.BlockSpec((SLOTS * t,), lambda i: (i,), memory_space=pltpu.SMEM),
            pl.BlockSpec((SLOTS * t,), lambda i: (i,), memory_space=pltpu.SMEM),
            pl.BlockSpec((t * SUBLANES, LANES), lambda i: (i, 0)),
            pl.BlockSpec(v_packed.shape, lambda i: (0, 0), pipeline_mode=pl.Buffered(1)),
        ],
        out_specs=pl.BlockSpec((t * SUBLANES, LANES), lambda i: (i, 0)),
        out_shape=jax.ShapeDtypeStruct(x_rows.shape, jnp.float32),
        compiler_params=_params(1, V7X_VMEM_LIMIT_BYTES),
        name="peer_out",
    )(idx_flat, w_flat, x_rows, v_packed)


def _final_norm_kernel(x_ref, g_ref, o_ref):
    o_ref[...] = _rms(x_ref[...], g_ref[...])


def _final_norm(x, g):
    n, d = x.shape
    tm = TOKEN_TILE
    return pl.pallas_call(
        _final_norm_kernel,
        grid=(n // tm,),
        in_specs=[pl.BlockSpec((tm, d), lambda i: (i, 0)), pl.BlockSpec((1, d), lambda i: (0, 0))],
        out_specs=pl.BlockSpec((tm, d), lambda i: (i, 0)),
        out_shape=jax.ShapeDtypeStruct((n, d), jnp.float32),
        compiler_params=_params(1),
        name="final_norm",
    )(x, g.reshape(1, d))


def peer_layer(x, g, w_query, sub_keys, u_packed, v_packed, g_final=None):
    n, d = x.shape
    h, idx, gate = _peer_select(x, g, w_query.astype(jnp.bfloat16), sub_keys.astype(jnp.bfloat16))
    idx_flat = idx.reshape(-1)
    w = _peer_act(idx_flat, h.reshape(n * SUBLANES, LANES), u_packed, gate)
    y = _peer_out(idx_flat, w.reshape(-1), x.reshape(n * SUBLANES, LANES), v_packed).reshape(n, d)
    return y if g_final is None else _final_norm(y, g_final)


def _trunk(x, seq_len, l0, l1, g_final):
    b = x.shape[0]
    x = x.reshape(b * seq_len, D_MODEL)
    x = conv_mixer_layer(x, l0["norm_mix"], l0["w_in"], l0["conv_w"], l0["w_out"], seq_len)
    x = peer_layer(x, l0["norm_ffn"], l0["w_query"], l0["sub_keys"], l0["u"], l0["v"])
    x = attn_mixer_layer(x, l1["norm_mix"], l1["w_qkv"], l1["w_o"], seq_len)
    x = peer_layer(x, l1["norm_ffn"], l1["w_query"], l1["sub_keys"], l1["u"], l1["v"], g_final)
    return x.reshape(b, seq_len, D_MODEL)


def kernel(x_prompt, x_sample, l0_norm_mix, l0_conv_w_in, l0_conv_w, l0_conv_w_out, l0_norm_ffn, l0_peer_w_query, l0_peer_sub_keys, l0_peer_u, l0_peer_v, l1_norm_mix, l1_attn_w_qkv, l1_attn_w_o, l1_norm_ffn, l1_peer_w_query, l1_peer_sub_keys, l1_peer_u, l1_peer_v, norm_final):
    l0 = dict(norm_mix=l0_norm_mix, w_in=l0_conv_w_in, conv_w=l0_conv_w, w_out=l0_conv_w_out,
              norm_ffn=l0_norm_ffn, w_query=l0_peer_w_query, sub_keys=l0_peer_sub_keys,
              u=_pack_table(l0_peer_u), v=_pack_table(l0_peer_v))
    l1 = dict(norm_mix=l1_norm_mix, w_qkv=l1_attn_w_qkv, w_o=l1_attn_w_o,
              norm_ffn=l1_norm_ffn, w_query=l1_peer_w_query, sub_keys=l1_peer_sub_keys,
              u=_pack_table(l1_peer_u), v=_pack_table(l1_peer_v))
    return (_trunk(x_prompt, x_prompt.shape[1], l0, l1, norm_final),
            _trunk(x_sample, x_sample.shape[1], l0, l1, norm_final))
```

```python
import functools

import jax
import jax.numpy as jnp
from jax import lax
from jax.experimental import pallas as pl
from jax.experimental.pallas import tpu as pltpu

D_MODEL = 1024
RMS_EPS = 1e-6
TOKEN_TILE = 512
V7X_VMEM_LIMIT_BYTES = 56 * 1024 * 1024


def _params(n_grid_dims, vmem=None):
    return pltpu.CompilerParams(
        dimension_semantics=("arbitrary",) * n_grid_dims,
        vmem_limit_bytes=vmem)


def _rms(x, g):
    return x * lax.rsqrt(jnp.mean(x * x, axis=-1, keepdims=True) + RMS_EPS) * g


def _conv_in_kernel(x_ref, g_ref, w_ref, b_ref, z_ref):
    h = _rms(x_ref[...], g_ref[...]).astype(jnp.bfloat16)
    y = jnp.dot(h, w_ref[...], preferred_element_type=jnp.float32)
    d = D_MODEL
    b_ref[...] = y[:, :d]
    z_ref[...] = y[:, d:2 * d] * y[:, 2 * d:]


def _conv_in(x, g, w_in_bf16):
    n, d = x.shape
    tm = TOKEN_TILE
    return pl.pallas_call(
        _conv_in_kernel,
        grid=(n // tm,),
        in_specs=[
            pl.BlockSpec((tm, d), lambda i: (i, 0)),
            pl.BlockSpec((1, d), lambda i: (0, 0)),
            pl.BlockSpec((d, 3 * d), lambda i: (0, 0)),
        ],
        out_specs=[pl.BlockSpec((tm, d), lambda i: (i, 0)),
                   pl.BlockSpec((tm, d), lambda i: (i, 0))],
        out_shape=[jax.ShapeDtypeStruct((n, d), jnp.float32)] * 2,
        compiler_params=_params(1, V7X_VMEM_LIMIT_BYTES),
        name="conv_in",
    )(x, g.reshape(1, d), w_in_bf16)


def _conv_out_kernel(tiles_per_seq, x_ref, b_ref, z_ref, zp_ref, zn_ref, cw_ref, w_ref, o_ref):
    i = pl.program_id(0)
    pos = i % tiles_per_seq
    z = z_ref[...]
    tm = z.shape[0]
    row = lax.broadcasted_iota(jnp.int32, z.shape, 0)
    prev_row = jnp.where(pos == 0, 0.0, zp_ref[7:8, :])
    next_row = jnp.where(pos == tiles_per_seq - 1, 0.0, zn_ref[0:1, :])
    z_prev = jnp.where(row == 0, prev_row, pltpu.roll(z, 1, 0))
    z_next = jnp.where(row == tm - 1, next_row, pltpu.roll(z, tm - 1, 0))
    cw = cw_ref[...]
    conv = z_prev * cw[0:1, :] + z * cw[1:2, :] + z_next * cw[2:3, :]
    a = (b_ref[...] * conv).astype(jnp.bfloat16)
    o_ref[...] = x_ref[...] + jnp.dot(a, w_ref[...], preferred_element_type=jnp.float32)


def _conv_out(x, b, z, conv_w, w_out_bf16, seq_len):
    n, d = x.shape
    tm = TOKEN_TILE
    tps = seq_len // tm
    sub = tm // 8
    nb8 = n // 8
    tile = lambda i: (i, 0)
    return pl.pallas_call(
        functools.partial(_conv_out_kernel, tps),
        grid=(n // tm,),
        in_specs=[
            pl.BlockSpec((tm, d), tile),
            pl.BlockSpec((tm, d), tile),
            pl.BlockSpec((tm, d), tile),
            pl.BlockSpec((8, d), lambda i: (jnp.maximum(i * sub - 1, 0), 0)),
            pl.BlockSpec((8, d), lambda i: (jnp.minimum((i + 1) * sub, nb8 - 1), 0)),
            pl.BlockSpec((3, d), lambda i: (0, 0)),
            pl.BlockSpec((d, d), lambda i: (0, 0)),
        ],
        out_specs=pl.BlockSpec((tm, d), tile),
        out_shape=jax.ShapeDtypeStruct((n, d), jnp.float32),
        compiler_params=_params(1, V7X_VMEM_LIMIT_BYTES),
        name="conv_out",
    )(x, b, z, z, z, conv_w, w_out_bf16)


def conv_mixer_layer(x, g, w_in, conv_w, w_out, seq_len):
    b, z = _conv_in(x, g, w_in.astype(jnp.bfloat16))
    return _conv_out(x, b, z, conv_w, w_out.astype(jnp.bfloat16), seq_len)


DILATED_CONFIGS = ((128, 1), (512, 4), (2048, 16))
N_GROUPS = len(DILATED_CONFIGS)
HEADS_PER_GROUP = 16
HEAD_DIM = D_MODEL // HEADS_PER_GROUP
ATTN_SIDE = 64
ATTN_Q_BLOCK = 128
LSE_LANES = 128


def _qkv_kernel(x_ref, g_ref, w_ref, o_ref):
    h = _rms(x_ref[...], g_ref[...]).astype(jnp.bfloat16)
    d = D_MODEL
    for c in range(o_ref.shape[1] // d):
        o_ref[:, c * d:(c + 1) * d] = jnp.dot(
            h, w_ref[:, c * d:(c + 1) * d], preferred_element_type=jnp.float32).astype(jnp.bfloat16)


def _qkv_proj(x, g, w_bf16):
    n, d = x.shape
    tm = TOKEN_TILE
    c = w_bf16.shape[1]
    return pl.pallas_call(
        _qkv_kernel,
        grid=(n // tm,),
        in_specs=[
            pl.BlockSpec((tm, d), lambda i: (i, 0)),
            pl.BlockSpec((1, d), lambda i: (0, 0)),
            pl.BlockSpec((d, c), lambda i: (0, 0), pipeline_mode=pl.Buffered(1)),
        ],
        out_specs=pl.BlockSpec((tm, c), lambda i: (i, 0)),
        out_shape=jax.ShapeDtypeStruct((n, c), jnp.bfloat16),
        compiler_params=_params(1, V7X_VMEM_LIMIT_BYTES),
        name="qkv_proj",
    )(x, g.reshape(1, d), w_bf16)


def _alibi_slope(head):
    n = N_GROUPS * HEADS_PER_GROUP
    return 2.0 ** (-8.0 * (head + 1) / n)


def _attn_kernel(group, dilation, seq_strided, q_ref, kp_ref, kc_ref, kn_ref,
                 vp_ref, vc_ref, vn_ref, o_ref, lse_ref):
    j = pl.program_id(2)
    qb = ATTN_Q_BLOCK
    nk = qb + 2 * ATTN_SIDE
    qi = lax.broadcasted_iota(jnp.int32, (qb, nk), 0)
    ci = lax.broadcasted_iota(jnp.int32, (qb, nk), 1)
    rel = ci - ATTN_SIDE - qi
    key_pos = j * qb - ATTN_SIDE + ci
    valid = (jnp.abs(rel) <= ATTN_SIDE) & (key_pos >= 0) & (key_pos < seq_strided)
    dist = (dilation * jnp.abs(rel)).astype(jnp.float32)
    lane = lax.broadcasted_iota(jnp.int32, (qb, LSE_LANES), 1)
    lse_all = jnp.zeros((qb, LSE_LANES), jnp.float32)
    for h in range(HEADS_PER_GROUP):
        cols = slice(h * HEAD_DIM, (h + 1) * HEAD_DIM)
        q = q_ref[0, :, cols]
        k = jnp.concatenate([kp_ref[0, :, cols], kc_ref[0, :, cols], kn_ref[0, :, cols]], axis=0)
        v = jnp.concatenate([vp_ref[0, :, cols], vc_ref[0, :, cols], vn_ref[0, :, cols]], axis=0)
        s = lax.dot_general(q, k, (((1,), (1,)), ((), ())), preferred_element_type=jnp.float32)
        slope = _alibi_slope(group * HEADS_PER_GROUP + h)
        s = s * (HEAD_DIM ** -0.5) - slope * dist
        s = jnp.where(valid, s, -jnp.inf)
        m = jnp.max(s, axis=-1, keepdims=True)
        p = jnp.exp(s - m)
        den = jnp.sum(p, axis=-1, keepdims=True)
        o = jnp.dot(p.astype(jnp.bfloat16), v, preferred_element_type=jnp.float32)
        o_ref[0, :, cols] = o / den
        lse_all = jnp.where(lane == h, m + jnp.log(den), lse_all)
    lse_ref[0] = lse_all


def _group_attention(qkv, group, dilation, batch, seq_len):
    d = D_MODEL
    ls = seq_len // dilation
    width = qkv.shape[1] // d
    qb = ATTN_Q_BLOCK
    sb = qb // ATTN_SIDE
    n_side = ls // ATTN_SIDE
    view = qkv.reshape(batch, ls, dilation * width * d)
    col0 = group * 3

    def cur(c):
        return pl.BlockSpec((1, qb, d), lambda b, r, j: (b, j, r * width + col0 + c))

    def prev(c):
        return pl.BlockSpec((1, ATTN_SIDE, d),
                            lambda b, r, j: (b, jnp.maximum(j * sb - 1, 0), r * width + col0 + c))

    def nxt(c):
        return pl.BlockSpec((1, ATTN_SIDE, d),
                            lambda b, r, j: (b, jnp.minimum((j + 1) * sb, n_side - 1), r * width + col0 + c))

    o, lse = pl.pallas_call(
        functools.partial(_attn_kernel, group, dilation, ls),
        grid=(batch, dilation, ls // qb),
        in_specs=[cur(0), prev(1), cur(1), nxt(1), prev(2), cur(2), nxt(2)],
        out_specs=[pl.BlockSpec((1, qb, d), lambda b, r, j: (b, j, r)),
                   pl.BlockSpec((1, qb, LSE_LANES), lambda b, r, j: (b, j, r))],
        out_shape=[jax.ShapeDtypeStruct((batch, ls, dilation * d), jnp.float32),
                   jax.ShapeDtypeStruct((batch, ls, dilation * LSE_LANES), jnp.float32)],
        compiler_params=_params(3, V7X_VMEM_LIMIT_BYTES),
        name=f"attn_g{group}",
    )(view, view, view, view, view, view, view)
    n = batch * seq_len
    return o.reshape(n, d), lse.reshape(n, LSE_LANES)


def _merge_kernel(x_ref, o0_ref, o1_ref, o2_ref, l0_ref, l1_ref, l2_ref, e_ref, w_ref, out_ref):
    l0, l1, l2 = l0_ref[...], l1_ref[...], l2_ref[...]
    m = jnp.maximum(jnp.maximum(l0, l1), l2)
    e0, e1, e2 = jnp.exp(l0 - m), jnp.exp(l1 - m), jnp.exp(l2 - m)
    tot = e0 + e1 + e2
    expand = lambda a: jnp.dot(a, e_ref[...], preferred_element_type=jnp.float32,
                               precision=lax.Precision.HIGHEST)
    merged = (expand(e0 / tot) * o0_ref[...] + expand(e1 / tot) * o1_ref[...]
              + expand(e2 / tot) * o2_ref[...])
    out_ref[...] = x_ref[...] + jnp.dot(merged.astype(jnp.bfloat16), w_ref[...],
                                        preferred_element_type=jnp.float32)


def _attn_merge(x, outs, lses, w_o_bf16):
    n, d = x.shape
    tm = TOKEN_TILE
    head_of_col = jnp.arange(d, dtype=jnp.int32) // HEAD_DIM
    expand = (jnp.arange(LSE_LANES, dtype=jnp.int32)[:, None] == head_of_col[None, :]).astype(jnp.float32)
    tile = lambda i: (i, 0)
    const = lambda i: (0, 0)
    return pl.pallas_call(
        _merge_kernel,
        grid=(n // tm,),
        in_specs=[pl.BlockSpec((tm, d), tile)] * 4 + [pl.BlockSpec((tm, LSE_LANES), tile)] * 3
        + [pl.BlockSpec((LSE_LANES, d), const), pl.BlockSpec((d, d), const)],
        out_specs=pl.BlockSpec((tm, d), tile),
        out_shape=jax.ShapeDtypeStruct((n, d), jnp.float32),
        compiler_params=_params(1, V7X_VMEM_LIMIT_BYTES),
        name="attn_merge",
    )(x, *outs, *lses, expand, w_o_bf16)


def attn_mixer_layer(x, g, w_qkv, w_o, seq_len):
    n = x.shape[0]
    batch = n // seq_len
    qkv = _qkv_proj(x, g, w_qkv.astype(jnp.bfloat16))
    outs, lses = [], []
    for group, (window, dilation) in enumerate(DILATED_CONFIGS):
        assert window // (2 * dilation) == ATTN_SIDE
        o, lse = _group_attention(qkv, group, dilation, batch, seq_len)
        outs.append(o)
        lses.append(lse)
    return _attn_merge(x, outs, lses, w_o.astype(jnp.bfloat16))


PEER_HEADS = 8
PEER_HALF = 128
N_KEYS = 128
PEER_TOPK = 16
PEER_TILE = 128
SLOTS = PEER_HEADS * PEER_TOPK
LANES = 128
HALF_D = D_MODEL // 2
ROWS_PER_EXPERT = HALF_D // LANES
TABLE_PAD_ROWS = ROWS_PER_EXPERT
SUBLANES = 8


def _take_top(s, ids, n, payload=None):
    vals, picks = [], []
    for _ in range(n):
        m = jnp.max(s, axis=0, keepdims=True)
        sel = jnp.min(jnp.where(s == m, ids, jnp.int32(2 ** 30)), axis=0, keepdims=True)
        hit = ids == sel
        if payload is None:
            picks.append(sel)
        else:
            picks.append(jnp.max(jnp.where(hit, payload, -1), axis=0, keepdims=True))
        s = jnp.where(hit, -jnp.inf, s)
        vals.append(m)
    return vals, picks


def _peer_select_kernel(x_ref, g_ref, wq_ref, keys_ref, h_ref, idx_ref, gate_ref, q_scr, idx_scr):
    hd = pl.program_id(1)
    t = x_ref.shape[0]

    @pl.when(hd == 0)
    def _():
        h = _rms(x_ref[...], g_ref[...])
        h_ref[...] = h
        q = jnp.dot(h.astype(jnp.bfloat16), wq_ref[...], preferred_element_type=jnp.float32)
        for k in range(PEER_HEADS):
            q_scr[k] = q[:, k * 2 * PEER_HALF:(k + 1) * 2 * PEER_HALF].astype(jnp.bfloat16)

    q = q_scr[hd]
    key_id = lax.broadcasted_iota(jnp.int32, (N_KEYS, t), 0)
    halves = []
    for p in range(2):
        s = lax.dot_general(keys_ref[0, p], q[:, p * PEER_HALF:(p + 1) * PEER_HALF],
                            (((1,), (1,)), ((), ())), preferred_element_type=jnp.float32)
        vals, ids = _take_top(s, key_id, PEER_TOPK)
        halves.append((jnp.concatenate(vals, axis=0), jnp.concatenate(ids, axis=0)))
    (s1, i1), (s2, i2) = halves

    cv, cp, ci = [], [], []
    for b in range(SUBLANES):
        rows = PEER_TOPK if b == 0 else SUBLANES
        a_id = lax.broadcasted_iota(jnp.int32, (rows, t), 0)
        v = s1[0:rows] + s2[b:b + 1]
        na = PEER_TOPK // (b + 1)
        if na < rows:
            v = jnp.where(a_id < na, v, -jnp.inf)
        cv.append(v)
        cp.append(a_id * PEER_TOPK + b)
        ci.append(i1[0:rows] * N_KEYS + i2[b:b + 1])
    b_id = lax.broadcasted_iota(jnp.int32, (SUBLANES, t), 0) + SUBLANES
    cv.append(s1[0:1] + s2[SUBLANES:])
    cp.append(b_id)
    ci.append(i1[0:1] * N_KEYS + i2[SUBLANES:])
    top, experts = _take_top(jnp.concatenate(cv, axis=0), jnp.concatenate(cp, axis=0), PEER_TOPK,
                             payload=jnp.concatenate(ci, axis=0))
    top = jnp.concatenate(top, axis=0)
    e = jnp.exp(top - top[0:1])
    gate_ref[0] = e / jnp.sum(e, axis=0, keepdims=True)
    rows = pl.ds(pl.multiple_of(hd * PEER_TOPK, PEER_TOPK), PEER_TOPK)
    idx_scr[rows, :] = jnp.concatenate(experts, axis=0) * ROWS_PER_EXPERT + TABLE_PAD_ROWS

    @pl.when(hd == PEER_HEADS - 1)
    def _():
        idx_ref[...] = idx_scr[...].T


def _peer_select(x, g, wq_bf16, keys_bf16):
    n, d = x.shape
    t = PEER_TILE
    nt = n // t
    return pl.pallas_call(
        _peer_select_kernel,
        grid=(nt, PEER_HEADS),
        in_specs=[
            pl.BlockSpec((t, d), lambda i, h: (i, 0)),
            pl.BlockSpec((1, d), lambda i, h: (0, 0)),
            pl.BlockSpec(wq_bf16.shape, lambda i, h: (0, 0)),
            pl.BlockSpec((1, 2, N_KEYS, PEER_HALF), lambda i, h: (h, 0, 0, 0)),
        ],
        out_specs=[pl.BlockSpec((t, d), lambda i, h: (i, 0)),
                   pl.BlockSpec((t, SLOTS), lambda i, h: (i, 0)),
                   pl.BlockSpec((1, PEER_TOPK, t), lambda i, h: (i, h, 0))],
        out_shape=[jax.ShapeDtypeStruct((n, d), jnp.float32),
                   jax.ShapeDtypeStruct((n, SLOTS), jnp.int32),
                   jax.ShapeDtypeStruct((nt, SLOTS, t), jnp.float32)],
        scratch_shapes=[pltpu.VMEM((PEER_HEADS, t, 2 * PEER_HALF), jnp.bfloat16),
                        pltpu.VMEM((SLOTS, t), jnp.int32)],
        compiler_params=_params(2, V7X_VMEM_LIMIT_BYTES),
        name="peer_select",
    )(x, g.reshape(1, d), wq_bf16, keys_bf16)


def _pack_table(tab):
    bits = lax.bitcast_convert_type(tab.astype(jnp.bfloat16), jnp.uint16).astype(jnp.uint32)
    packed = bits[:, :HALF_D] | (bits[:, HALF_D:] << 16)
    packed = packed.reshape(tab.shape[0] * ROWS_PER_EXPERT, LANES)
    return jnp.pad(packed, ((TABLE_PAD_ROWS, TABLE_PAD_ROWS), (0, 0)))


def _load_expert_pair(tab_ref, head_idx_ref, base, low_sublanes):
    ra = pl.multiple_of(head_idx_ref[base], ROWS_PER_EXPERT)
    rb = pl.multiple_of(head_idx_ref[base + 1] - ROWS_PER_EXPERT, ROWS_PER_EXPERT)
    pair = jnp.where(low_sublanes, tab_ref[pl.ds(ra, SUBLANES), :], tab_ref[pl.ds(rb, SUBLANES), :])
    lo = pltpu.bitcast(pair << 16, jnp.float32)
    hi = pltpu.bitcast(pair & jnp.uint32(0xFFFF0000), jnp.float32)
    return lo, hi


def _expert_pairs(tab_ref, idx_refs, t, low_sublanes):
    for kk in range(PEER_TOPK // 2):
        for hd in range(PEER_HEADS):
            lo, hi = _load_expert_pair(tab_ref, idx_refs[hd], t * PEER_TOPK + 2 * kk, low_sublanes)
            yield hd * (PEER_TOPK // 2) + kk, lo, hi


def _to_token_rows(src_ref, rows_scr):
    t = src_ref.shape[0]
    for c in range(SUBLANES):
        rows_scr[pl.ds(c, t, stride=SUBLANES), :] = src_ref[:, c * LANES:(c + 1) * LANES]


def _peer_act_kernel(*refs):
    idx_refs = refs[:PEER_HEADS]
    h_ref, tab_ref, gate_ref, w_ref, hrow_scr, prod_a, prod_b = refs[PEER_HEADS:]
    t_tile = PEER_TILE
    lane = lax.broadcasted_iota(jnp.int32, (SLOTS, t_tile), 1)
    low = lax.broadcasted_iota(jnp.int32, (SUBLANES, LANES), 0) < ROWS_PER_EXPERT
    _to_token_rows(h_ref, hrow_scr)
    prod_b[...] = jnp.zeros(prod_b.shape, jnp.float32)

    def products(t, prod_scr):
        hv = hrow_scr[pl.ds(pl.multiple_of(t * SUBLANES, SUBLANES), SUBLANES), :]
        swapped = pltpu.roll(hv, ROWS_PER_EXPERT, 0)
        h_lo = jnp.where(low, hv, swapped)
        h_hi = jnp.where(low, swapped, hv)
        for p, lo, hi in _expert_pairs(tab_ref, idx_refs, t, low):
            prod_scr[p * SUBLANES:(p + 1) * SUBLANES, :] = lo * h_lo + hi * h_hi

    def reduce_into(act, t, prod_scr):
        part = prod_scr[pl.ds(0, SLOTS, stride=ROWS_PER_EXPERT), :]
        for k in range(1, ROWS_PER_EXPERT):
            part = part + prod_scr[pl.ds(k, SLOTS, stride=ROWS_PER_EXPERT), :]
        col = jnp.sum(part, axis=1, keepdims=True)
        return jnp.where(lane == t, col, act)

    def two_tokens(i, act):
        t0 = 2 * i
        products(t0, prod_a)
        act = reduce_into(act, t0 - 1, prod_b)
        products(t0 + 1, prod_b)
        return reduce_into(act, t0, prod_a)

    act = lax.fori_loop(0, t_tile // 2, two_tokens, jnp.zeros((SLOTS, t_tile), jnp.float32))
    act = reduce_into(act, t_tile - 1, prod_b)
    gelu = 0.5 * act * (1.0 + lax.erf(act * (0.5 ** 0.5)))
    w_ref[0] = gate_ref[0] * gelu


def _head_idx_specs():
    return [pl.BlockSpec((PEER_TOPK * PEER_TILE,), lambda i: (i,), memory_space=pltpu.SMEM)] * PEER_HEADS


def _peer_act(head_idx, h, u_packed, gate):
    nt = gate.shape[0]
    t = PEER_TILE
    d = h.shape[1]
    prod = pltpu.VMEM((SLOTS * ROWS_PER_EXPERT, LANES), jnp.float32)
    return pl.pallas_call(
        _peer_act_kernel,
        grid=(nt,),
        in_specs=_head_idx_specs() + [
            pl.BlockSpec((t, d), lambda i: (i, 0)),
            pl.BlockSpec(u_packed.shape, lambda i: (0, 0), pipeline_mode=pl.Buffered(1)),
            pl.BlockSpec((1, SLOTS, t), lambda i: (i, 0, 0)),
        ],
        out_specs=pl.BlockSpec((1, SLOTS, t), lambda i: (i, 0, 0)),
        out_shape=jax.ShapeDtypeStruct((nt, SLOTS, t), jnp.float32),
        scratch_shapes=[pltpu.VMEM((t * SUBLANES, LANES), jnp.float32), prod, prod],
        compiler_params=_params(1, V7X_VMEM_LIMIT_BYTES),
        name="peer_act",
    )(*head_idx, h, u_packed, gate)


def _peer_out_kernel(final_norm, *refs):
    idx_refs = refs[:PEER_HEADS]
    w_ref, x_ref, tab_ref, g_ref, o_ref, xrow_scr, wb_a, wb_b = refs[PEER_HEADS:]
    t_tile = PEER_TILE
    lane = lax.broadcasted_iota(jnp.int32, (SLOTS, t_tile), 1)
    low = lax.broadcasted_iota(jnp.int32, (SUBLANES, LANES), 0) < ROWS_PER_EXPERT
    _to_token_rows(x_ref, xrow_scr)

    def spread_weights(t, wb_scr):
        w_col = jnp.sum(jnp.where(lane == t, w_ref[0], 0.0), axis=1, keepdims=True)
        wb_scr[...] = jnp.broadcast_to(w_col, (SLOTS, LANES))

    def accumulate(t, wb_scr):
        acc_lo = jnp.zeros((SUBLANES, LANES), jnp.float32)
        acc_hi = jnp.zeros((SUBLANES, LANES), jnp.float32)
        for p, lo, hi in _expert_pairs(tab_ref, idx_refs, t, low):
            w2 = jnp.where(low, jnp.broadcast_to(wb_scr[2 * p:2 * p + 1, :], (SUBLANES, LANES)),
                           jnp.broadcast_to(wb_scr[2 * p + 1:2 * p + 2, :], (SUBLANES, LANES)))
            acc_lo = acc_lo + w2 * lo
            acc_hi = acc_hi + w2 * hi
        y = jnp.where(low, acc_lo + pltpu.roll(acc_lo, ROWS_PER_EXPERT, 0),
                      acc_hi + pltpu.roll(acc_hi, ROWS_PER_EXPERT, 0))
        rows = pl.ds(pl.multiple_of(t * SUBLANES, SUBLANES), SUBLANES)
        xrow_scr[rows, :] = xrow_scr[rows, :] + y

    spread_weights(0, wb_a)

    def two_tokens(i, carry):
        t0 = 2 * i
        spread_weights(t0 + 1, wb_b)
        accumulate(t0, wb_a)
        spread_weights(t0 + 2, wb_a)
        accumulate(t0 + 1, wb_b)
        return carry

    lax.fori_loop(0, t_tile // 2, two_tokens, 0)
    for c in range(SUBLANES):
        o_ref[:, c * LANES:(c + 1) * LANES] = xrow_scr[pl.ds(c, t_tile, stride=SUBLANES), :]
    if final_norm:
        o_ref[...] = _rms(o_ref[...], g_ref[...])


def _peer_out(head_idx, w, x, v_packed, g_final):
    t = PEER_TILE
    n, d = x.shape
    nt = n // t
    g = jnp.ones((d,), jnp.float32) if g_final is None else g_final
    wb = pltpu.VMEM((SLOTS, LANES), jnp.float32)
    return pl.pallas_call(
        functools.partial(_peer_out_kernel, g_final is not None),
        grid=(nt,),
        in_specs=_head_idx_specs() + [
            pl.BlockSpec((1, SLOTS, t), lambda i: (i, 0, 0)),
            pl.BlockSpec((t, d), lambda i: (i, 0)),
            pl.BlockSpec(v_packed.shape, lambda i: (0, 0), pipeline_mode=pl.Buffered(1)),
            pl.BlockSpec((1, d), lambda i: (0, 0)),
        ],
        out_specs=pl.BlockSpec((t, d), lambda i: (i, 0)),
        out_shape=jax.ShapeDtypeStruct((n, d), jnp.float32),
        scratch_shapes=[pltpu.VMEM((t * SUBLANES, LANES), jnp.float32), wb, wb],
        compiler_params=_params(1, V7X_VMEM_LIMIT_BYTES),
        name="peer_out",
    )(*head_idx, w, x, v_packed, g.reshape(1, d))


def peer_layer(x, g, w_query, sub_keys, u_packed, v_packed, g_final=None):
    n = x.shape[0]
    h, idx, gate = _peer_select(x, g, w_query.astype(jnp.bfloat16), sub_keys.astype(jnp.bfloat16))
    by_head = idx.reshape(n, PEER_HEADS, PEER_TOPK).transpose(1, 0, 2).reshape(PEER_HEADS, n * PEER_TOPK)
    head_idx = [by_head[hd] for hd in range(PEER_HEADS)]
    w = _peer_act(head_idx, h, u_packed, gate)
    return _peer_out(head_idx, w, x, v_packed, g_final)


def _trunk(x, seq_len, l0, l1, g_final):
    b = x.shape[0]
    x = x.reshape(b * seq_len, D_MODEL)
    x = conv_mixer_layer(x, l0["norm_mix"], l0["w_in"], l0["conv_w"], l0["w_out"], seq_len)
    x = peer_layer(x, l0["norm_ffn"], l0["w_query"], l0["sub_keys"], l0["u"], l0["v"])
    x = attn_mixer_layer(x, l1["norm_mix"], l1["w_qkv"], l1["w_o"], seq_len)
    x = peer_layer(x, l1["norm_ffn"], l1["w_query"], l1["sub_keys"], l1["u"], l1["v"], g_final)
    return x.reshape(b, seq_len, D_MODEL)


def kernel(x_prompt, x_sample, l0_norm_mix, l0_conv_w_in, l0_conv_w, l0_conv_w_out, l0_norm_ffn, l0_peer_w_query, l0_peer_sub_keys, l0_peer_u, l0_peer_v, l1_norm_mix, l1_attn_w_qkv, l1_attn_w_o, l1_norm_ffn, l1_peer_w_query, l1_peer_sub_keys, l1_peer_u, l1_peer_v, norm_final):
    l0 = dict(norm_mix=l0_norm_mix, w_in=l0_conv_w_in, conv_w=l0_conv_w, w_out=l0_conv_w_out,
              norm_ffn=l0_norm_ffn, w_query=l0_peer_w_query, sub_keys=l0_peer_sub_keys,
              u=_pack_table(l0_peer_u), v=_pack_table(l0_peer_v))
    l1 = dict(norm_mix=l1_norm_mix, w_qkv=l1_attn_w_qkv, w_o=l1_attn_w_o,
              norm_ffn=l1_norm_ffn, w_query=l1_peer_w_query, sub_keys=l1_peer_sub_keys,
              u=_pack_table(l1_peer_u), v=_pack_table(l1_peer_v))
    return (_trunk(x_prompt, x_prompt.shape[1], l0, l1, norm_final),
            _trunk(x_sample, x_sample.shape[1], l0, l1, norm_final))
```

```python
import functools

import jax
import jax.numpy as jnp
from jax import lax
from jax.experimental import pallas as pl
from jax.experimental.pallas import tpu as pltpu

D_MODEL = 1024
RMS_EPS = 1e-6
TOKEN_TILE = 512
V7X_VMEM_LIMIT_BYTES = 56 * 1024 * 1024


def _params(n_grid_dims, vmem=None):
    return pltpu.CompilerParams(
        dimension_semantics=("arbitrary",) * n_grid_dims,
        vmem_limit_bytes=vmem)


def _rms(x, g):
    return x * lax.rsqrt(jnp.mean(x * x, axis=-1, keepdims=True) + RMS_EPS) * g


def _conv_in_kernel(x_ref, g_ref, w_ref, b_ref, z_ref):
    h = _rms(x_ref[...], g_ref[...]).astype(jnp.bfloat16)
    y = jnp.dot(h, w_ref[...], preferred_element_type=jnp.float32)
    d = D_MODEL
    b_ref[...] = y[:, :d]
    z_ref[...] = y[:, d:2 * d] * y[:, 2 * d:]


def _conv_in(x, g, w_in_bf16):
    n, d = x.shape
    tm = TOKEN_TILE
    return pl.pallas_call(
        _conv_in_kernel,
        grid=(n // tm,),
        in_specs=[
            pl.BlockSpec((tm, d), lambda i: (i, 0)),
            pl.BlockSpec((1, d), lambda i: (0, 0)),
            pl.BlockSpec((d, 3 * d), lambda i: (0, 0)),
        ],
        out_specs=[pl.BlockSpec((tm, d), lambda i: (i, 0)),
                   pl.BlockSpec((tm, d), lambda i: (i, 0))],
        out_shape=[jax.ShapeDtypeStruct((n, d), jnp.float32)] * 2,
        compiler_params=_params(1, V7X_VMEM_LIMIT_BYTES),
        name="conv_in",
    )(x, g.reshape(1, d), w_in_bf16)


def _conv_out_kernel(tiles_per_seq, x_ref, b_ref, z_ref, zp_ref, zn_ref, cw_ref, w_ref, o_ref):
    i = pl.program_id(0)
    pos = i % tiles_per_seq
    z = z_ref[...]
    tm = z.shape[0]
    row = lax.broadcasted_iota(jnp.int32, z.shape, 0)
    prev_row = jnp.where(pos == 0, 0.0, zp_ref[7:8, :])
    next_row = jnp.where(pos == tiles_per_seq - 1, 0.0, zn_ref[0:1, :])
    z_prev = jnp.where(row == 0, prev_row, pltpu.roll(z, 1, 0))
    z_next = jnp.where(row == tm - 1, next_row, pltpu.roll(z, tm - 1, 0))
    cw = cw_ref[...]
    conv = z_prev * cw[0:1, :] + z * cw[1:2, :] + z_next * cw[2:3, :]
    a = (b_ref[...] * conv).astype(jnp.bfloat16)
    o_ref[...] = x_ref[...] + jnp.dot(a, w_ref[...], preferred_element_type=jnp.float32)


def _conv_out(x, b, z, conv_w, w_out_bf16, seq_len):
    n, d = x.shape
    tm = TOKEN_TILE
    tps = seq_len // tm
    sub = tm // 8
    nb8 = n // 8
    tile = lambda i: (i, 0)
    return pl.pallas_call(
        functools.partial(_conv_out_kernel, tps),
        grid=(n // tm,),
        in_specs=[
            pl.BlockSpec((tm, d), tile),
            pl.BlockSpec((tm, d), tile),
            pl.BlockSpec((tm, d), tile),
            pl.BlockSpec((8, d), lambda i: (jnp.maximum(i * sub - 1, 0), 0)),
            pl.BlockSpec((8, d), lambda i: (jnp.minimum((i + 1) * sub, nb8 - 1), 0)),
            pl.BlockSpec((3, d), lambda i: (0, 0)),
            pl.BlockSpec((d, d), lambda i: (0, 0)),
        ],
        out_specs=pl.BlockSpec((tm, d), tile),
        out_shape=jax.ShapeDtypeStruct((n, d), jnp.float32),
        compiler_params=_params(1, V7X_VMEM_LIMIT_BYTES),
        name="conv_out",
    )(x, b, z, z, z, conv_w, w_out_bf16)


def conv_mixer_layer(x, g, w_in, conv_w, w_out, seq_len):
    b, z = _conv_in(x, g, w_in.astype(jnp.bfloat16))
    return _conv_out(x, b, z, conv_w, w_out.astype(jnp.bfloat16), seq_len)


DILATED_CONFIGS = ((128, 1), (512, 4), (2048, 16))
N_GROUPS = len(DILATED_CONFIGS)
HEADS_PER_GROUP = 16
HEAD_DIM = D_MODEL // HEADS_PER_GROUP
ATTN_SIDE = 64
ATTN_Q_BLOCK = 128
LSE_LANES = 128


def _qkv_kernel(x_ref, g_ref, w_ref, *refs):
    out_refs, y_scr = refs[:N_GROUPS], refs[N_GROUPS]
    h = _rms(x_ref[...], g_ref[...]).astype(jnp.bfloat16)
    d = D_MODEL
    tm = x_ref.shape[0]
    for group, (_, dil) in enumerate(DILATED_CONFIGS):
        for c in range(3):
            col = (group * 3 + c) * d
            y = jnp.dot(h, w_ref[:, col:col + d], preferred_element_type=jnp.float32)
            if dil == 1:
                out_refs[group][0, :, c * d:(c + 1) * d] = y.astype(jnp.bfloat16)
                continue
            for k in range(d // LANES):
                y_scr[k] = y[:, k * LANES:(k + 1) * LANES]
            for r in range(dil):
                for k in range(d // LANES):
                    col = (r * 3 + c) * d + k * LANES
                    out_refs[group][0, :, col:col + LANES] = (
                        y_scr[k, pl.ds(r, tm // dil, stride=dil), :].astype(jnp.bfloat16))


def _qkv_proj(x, g, w_bf16, seq_len):
    n, d = x.shape
    tm = TOKEN_TILE
    tps = seq_len // tm
    batch = n // seq_len
    return pl.pallas_call(
        _qkv_kernel,
        grid=(n // tm,),
        in_specs=[
            pl.BlockSpec((tm, d), lambda i: (i, 0)),
            pl.BlockSpec((1, d), lambda i: (0, 0)),
            pl.BlockSpec(w_bf16.shape, lambda i: (0, 0), pipeline_mode=pl.Buffered(1)),
        ],
        out_specs=[pl.BlockSpec((1, tm // dil, dil * 3 * d), lambda i: (i // tps, i % tps, 0))
                   for _, dil in DILATED_CONFIGS],
        out_shape=[jax.ShapeDtypeStruct((batch, seq_len // dil, dil * 3 * d), jnp.bfloat16)
                   for _, dil in DILATED_CONFIGS],
        scratch_shapes=[pltpu.VMEM((d // LANES, tm, LANES), jnp.float32)],
        compiler_params=_params(1, V7X_VMEM_LIMIT_BYTES),
        name="qkv_proj",
    )(x, g.reshape(1, d), w_bf16)


def _alibi_slope(head):
    n = N_GROUPS * HEADS_PER_GROUP
    return 2.0 ** (-8.0 * (head + 1) / n)


def _attn_kernel(group, dilation, seq_strided, q_ref, kp_ref, kc_ref, kn_ref,
                 vp_ref, vc_ref, vn_ref, o_ref, lse_ref):
    j = pl.program_id(2)
    qb = ATTN_Q_BLOCK
    nk = qb + 2 * ATTN_SIDE
    qi = lax.broadcasted_iota(jnp.int32, (qb, nk), 0)
    ci = lax.broadcasted_iota(jnp.int32, (qb, nk), 1)
    rel = ci - ATTN_SIDE - qi
    key_pos = j * qb - ATTN_SIDE + ci
    valid = (jnp.abs(rel) <= ATTN_SIDE) & (key_pos >= 0) & (key_pos < seq_strided)
    dist = (dilation * jnp.abs(rel)).astype(jnp.float32)
    lane = lax.broadcasted_iota(jnp.int32, (qb, LSE_LANES), 1)
    lse_all = jnp.zeros((qb, LSE_LANES), jnp.float32)
    for h in range(HEADS_PER_GROUP):
        cols = slice(h * HEAD_DIM, (h + 1) * HEAD_DIM)
        q = q_ref[0, :, cols]
        k = jnp.concatenate([kp_ref[0, :, cols], kc_ref[0, :, cols], kn_ref[0, :, cols]], axis=0)
        v = jnp.concatenate([vp_ref[0, :, cols], vc_ref[0, :, cols], vn_ref[0, :, cols]], axis=0)
        s = lax.dot_general(q, k, (((1,), (1,)), ((), ())), preferred_element_type=jnp.float32)
        slope = _alibi_slope(group * HEADS_PER_GROUP + h)
        s = s * (HEAD_DIM ** -0.5) - slope * dist
        s = jnp.where(valid, s, -jnp.inf)
        m = jnp.max(s, axis=-1, keepdims=True)
        p = jnp.exp(s - m)
        den = jnp.sum(p, axis=-1, keepdims=True)
        o = jnp.dot(p.astype(jnp.bfloat16), v, preferred_element_type=jnp.float32)
        o_ref[0, :, cols] = o / den
        lse_all = jnp.where(lane == h, m + jnp.log(den), lse_all)
    lse_ref[0] = lse_all


def _group_attention(view, group, dilation):
    d = D_MODEL
    batch, ls = view.shape[0], view.shape[1]
    width = 3
    qb = ATTN_Q_BLOCK
    sb = qb // ATTN_SIDE
    n_side = ls // ATTN_SIDE
    col0 = 0

    def cur(c):
        return pl.BlockSpec((1, qb, d), lambda b, r, j: (b, j, r * width + col0 + c))

    def prev(c):
        return pl.BlockSpec((1, ATTN_SIDE, d),
                            lambda b, r, j: (b, jnp.maximum(j * sb - 1, 0), r * width + col0 + c))

    def nxt(c):
        return pl.BlockSpec((1, ATTN_SIDE, d),
                            lambda b, r, j: (b, jnp.minimum((j + 1) * sb, n_side - 1), r * width + col0 + c))

    return pl.pallas_call(
        functools.partial(_attn_kernel, group, dilation, ls),
        grid=(batch, dilation, ls // qb),
        in_specs=[cur(0), prev(1), cur(1), nxt(1), prev(2), cur(2), nxt(2)],
        out_specs=[pl.BlockSpec((1, qb, d), lambda b, r, j: (b, j, r)),
                   pl.BlockSpec((1, qb, LSE_LANES), lambda b, r, j: (b, j, r))],
        out_shape=[jax.ShapeDtypeStruct((batch, ls, dilation * d), jnp.float32),
                   jax.ShapeDtypeStruct((batch, ls, dilation * LSE_LANES), jnp.float32)],
        compiler_params=_params(3, V7X_VMEM_LIMIT_BYTES),
        name=f"attn_g{group}",
    )(view, view, view, view, view, view, view)


def _token_order(ref, width, dil, scr):
    if dil == 1:
        return ref[0]
    rows = ref.shape[1]
    chunks = width // LANES
    for r in range(dil):
        for k in range(chunks):
            col = r * width + k * LANES
            scr[k, pl.ds(r, rows, stride=dil), :] = ref[0, :, col:col + LANES]
    return jnp.concatenate([scr[k] for k in range(chunks)], axis=1)


def _merge_kernel(x_ref, o0_ref, o1_ref, o2_ref, l0_ref, l1_ref, l2_ref, e_ref, w_ref, out_ref,
                  o_scr, l_scr):
    dils = [dil for _, dil in DILATED_CONFIGS]
    l0, l1, l2 = [_token_order(ref, LSE_LANES, dil, l_scr)
                  for ref, dil in zip((l0_ref, l1_ref, l2_ref), dils)]
    m = jnp.maximum(jnp.maximum(l0, l1), l2)
    e0, e1, e2 = jnp.exp(l0 - m), jnp.exp(l1 - m), jnp.exp(l2 - m)
    tot = e0 + e1 + e2
    expand = lambda a: jnp.dot(a, e_ref[...], preferred_element_type=jnp.float32,
                               precision=lax.Precision.HIGHEST)
    merged = jnp.zeros(x_ref.shape, jnp.float32)
    for e, o_ref, dil in zip((e0, e1, e2), (o0_ref, o1_ref, o2_ref), dils):
        merged = merged + expand(e / tot) * _token_order(o_ref, D_MODEL, dil, o_scr)
    out_ref[...] = x_ref[...] + jnp.dot(merged.astype(jnp.bfloat16), w_ref[...],
                                        preferred_element_type=jnp.float32)


def _attn_merge(x, outs, lses, w_o_bf16, seq_len):
    n, d = x.shape
    tm = TOKEN_TILE
    tps = seq_len // tm
    head_of_col = jnp.arange(d, dtype=jnp.int32) // HEAD_DIM
    expand = (jnp.arange(LSE_LANES, dtype=jnp.int32)[:, None] == head_of_col[None, :]).astype(jnp.float32)
    tile = lambda i: (i, 0)
    const = lambda i: (0, 0)
    strided = lambda i: (i // tps, i % tps, 0)
    dils = [dil for _, dil in DILATED_CONFIGS]
    return pl.pallas_call(
        _merge_kernel,
        grid=(n // tm,),
        in_specs=[pl.BlockSpec((tm, d), tile)]
        + [pl.BlockSpec((1, tm // dil, dil * d), strided) for dil in dils]
        + [pl.BlockSpec((1, tm // dil, dil * LSE_LANES), strided) for dil in dils]
        + [pl.BlockSpec((LSE_LANES, d), const), pl.BlockSpec((d, d), const)],
        out_specs=pl.BlockSpec((tm, d), tile),
        out_shape=jax.ShapeDtypeStruct((n, d), jnp.float32),
        scratch_shapes=[pltpu.VMEM((d // LANES, tm, LANES), jnp.float32),
                        pltpu.VMEM((1, tm, LSE_LANES), jnp.float32)],
        compiler_params=_params(1, V7X_VMEM_LIMIT_BYTES),
        name="attn_merge",
    )(x, *outs, *lses, expand, w_o_bf16)


def attn_mixer_layer(x, g, w_qkv, w_o, seq_len):
    views = _qkv_proj(x, g, w_qkv.astype(jnp.bfloat16), seq_len)
    outs, lses = [], []
    for group, (window, dilation) in enumerate(DILATED_CONFIGS):
        assert window // (2 * dilation) == ATTN_SIDE
        o, lse = _group_attention(views[group], group, dilation)
        outs.append(o)
        lses.append(lse)
    return _attn_merge(x, outs, lses, w_o.astype(jnp.bfloat16), seq_len)


PEER_HEADS = 8
PEER_HALF = 128
N_KEYS = 128
PEER_TOPK = 16
PEER_TILE = 128
ACT_REDUCE_UNROLL = 16
SLOTS = PEER_HEADS * PEER_TOPK
LANES = 128
HALF_D = D_MODEL // 2
ROWS_PER_EXPERT = HALF_D // LANES
TABLE_PAD_ROWS = ROWS_PER_EXPERT
SUBLANES = 8


def _take_top(s, ids, n, payload=None):
    vals, picks = [], []
    for _ in range(n):
        m = jnp.max(s, axis=0, keepdims=True)
        sel = jnp.min(jnp.where(s == m, ids, jnp.int32(2 ** 30)), axis=0, keepdims=True)
        hit = ids == sel
        if payload is None:
            picks.append(sel)
        else:
            picks.append(jnp.max(jnp.where(hit, payload, -1), axis=0, keepdims=True))
        s = jnp.where(hit, -jnp.inf, s)
        vals.append(m)
    return vals, picks


def _peer_select_kernel(x_ref, g_ref, wq_ref, keys_ref, h_ref, idx_ref, gate_ref, q_scr, idx_scr):
    hd = pl.program_id(1)
    t = x_ref.shape[0]

    @pl.when(hd == 0)
    def _():
        h = _rms(x_ref[...], g_ref[...])
        h_ref[...] = h
        q = jnp.dot(h.astype(jnp.bfloat16), wq_ref[...], preferred_element_type=jnp.float32)
        for k in range(PEER_HEADS):
            q_scr[k] = q[:, k * 2 * PEER_HALF:(k + 1) * 2 * PEER_HALF].astype(jnp.bfloat16)

    q = q_scr[hd]
    key_id = lax.broadcasted_iota(jnp.int32, (N_KEYS, t), 0)
    halves = []
    for p in range(2):
        s = lax.dot_general(keys_ref[0, p], q[:, p * PEER_HALF:(p + 1) * PEER_HALF],
                            (((1,), (1,)), ((), ())), preferred_element_type=jnp.float32)
        vals, ids = _take_top(s, key_id, PEER_TOPK)
        halves.append((jnp.concatenate(vals, axis=0), jnp.concatenate(ids, axis=0)))
    (s1, i1), (s2, i2) = halves

    cv, cp, ci = [], [], []
    for b in range(SUBLANES):
        rows = PEER_TOPK if b == 0 else SUBLANES
        a_id = lax.broadcasted_iota(jnp.int32, (rows, t), 0)
        v = s1[0:rows] + s2[b:b + 1]
        na = PEER_TOPK // (b + 1)
        if na < rows:
            v = jnp.where(a_id < na, v, -jnp.inf)
        cv.append(v)
        cp.append(a_id * PEER_TOPK + b)
        ci.append(i1[0:rows] * N_KEYS + i2[b:b + 1])
    b_id = lax.broadcasted_iota(jnp.int32, (SUBLANES, t), 0) + SUBLANES
    cv.append(s1[0:1] + s2[SUBLANES:])
    cp.append(b_id)
    ci.append(i1[0:1] * N_KEYS + i2[SUBLANES:])
    top, experts = _take_top(jnp.concatenate(cv, axis=0), jnp.concatenate(cp, axis=0), PEER_TOPK,
                             payload=jnp.concatenate(ci, axis=0))
    top = jnp.concatenate(top, axis=0)
    e = jnp.exp(top - top[0:1])
    gate_ref[0] = e / jnp.sum(e, axis=0, keepdims=True)
    rows = pl.ds(pl.multiple_of(hd * PEER_TOPK, PEER_TOPK), PEER_TOPK)
    idx_scr[rows, :] = jnp.concatenate(experts, axis=0) * ROWS_PER_EXPERT + TABLE_PAD_ROWS

    @pl.when(hd == PEER_HEADS - 1)
    def _():
        idx_ref[...] = idx_scr[...].T


def _peer_select(x, g, wq_bf16, keys_bf16):
    n, d = x.shape
    t = PEER_TILE
    nt = n // t
    return pl.pallas_call(
        _peer_select_kernel,
        grid=(nt, PEER_HEADS),
        in_specs=[
            pl.BlockSpec((t, d), lambda i, h: (i, 0)),
            pl.BlockSpec((1, d), lambda i, h: (0, 0)),
            pl.BlockSpec(wq_bf16.shape, lambda i, h: (0, 0)),
            pl.BlockSpec((1, 2, N_KEYS, PEER_HALF), lambda i, h: (h, 0, 0, 0)),
        ],
        out_specs=[pl.BlockSpec((t, d), lambda i, h: (i, 0)),
                   pl.BlockSpec((t, SLOTS), lambda i, h: (i, 0)),
                   pl.BlockSpec((1, PEER_TOPK, t), lambda i, h: (i, h, 0))],
        out_shape=[jax.ShapeDtypeStruct((n, d), jnp.float32),
                   jax.ShapeDtypeStruct((n, SLOTS), jnp.int32),
                   jax.ShapeDtypeStruct((nt, SLOTS, t), jnp.float32)],
        scratch_shapes=[pltpu.VMEM((PEER_HEADS, t, 2 * PEER_HALF), jnp.bfloat16),
                        pltpu.VMEM((SLOTS, t), jnp.int32)],
        compiler_params=_params(2, V7X_VMEM_LIMIT_BYTES),
        name="peer_select",
    )(x, g.reshape(1, d), wq_bf16, keys_bf16)


def _pack_table(tab):
    bits = lax.bitcast_convert_type(tab.astype(jnp.bfloat16), jnp.uint16).astype(jnp.uint32)
    packed = bits[:, :HALF_D] | (bits[:, HALF_D:] << 16)
    packed = packed.reshape(tab.shape[0] * ROWS_PER_EXPERT, LANES)
    return jnp.pad(packed, ((TABLE_PAD_ROWS, TABLE_PAD_ROWS), (0, 0)))


def _load_expert_pair(tab_ref, head_idx_ref, base, low_sublanes):
    ra = pl.multiple_of(head_idx_ref[base], ROWS_PER_EXPERT)
    rb = pl.multiple_of(head_idx_ref[base + 1] - ROWS_PER_EXPERT, ROWS_PER_EXPERT)
    pair = jnp.where(low_sublanes, tab_ref[pl.ds(ra, SUBLANES), :], tab_ref[pl.ds(rb, SUBLANES), :])
    lo = pltpu.bitcast(pair << 16, jnp.float32)
    hi = pltpu.bitcast(pair & jnp.uint32(0xFFFF0000), jnp.float32)
    return lo, hi


def _expert_pairs(tab_ref, idx_refs, t, low_sublanes):
    for kk in range(PEER_TOPK // 2):
        for hd in range(PEER_HEADS):
            lo, hi = _load_expert_pair(tab_ref, idx_refs[hd], t * PEER_TOPK + 2 * kk, low_sublanes)
            yield hd * (PEER_TOPK // 2) + kk, lo, hi


def _to_token_rows(src_ref, rows_scr):
    t = src_ref.shape[0]
    for c in range(SUBLANES):
        rows_scr[pl.ds(c, t, stride=SUBLANES), :] = src_ref[:, c * LANES:(c + 1) * LANES]


def _peer_act_kernel(*refs):
    idx_refs = refs[:PEER_HEADS]
    h_ref, tab_ref, gate_ref, w_ref, hrow_scr, prod_a, prod_b, act_scr = refs[PEER_HEADS:]
    t_tile = PEER_TILE
    lane = lax.broadcasted_iota(jnp.int32, (SLOTS, t_tile), 1)
    low = lax.broadcasted_iota(jnp.int32, (SUBLANES, LANES), 0) < ROWS_PER_EXPERT
    _to_token_rows(h_ref, hrow_scr)
    prod_a[...] = jnp.zeros(prod_a.shape, jnp.float32)
    prod_b[...] = jnp.zeros(prod_b.shape, jnp.float32)
    act_scr[...] = jnp.zeros(act_scr.shape, jnp.float32)

    def products(t, prod_scr):
        hv = hrow_scr[pl.ds(pl.multiple_of(t * SUBLANES, SUBLANES), SUBLANES), :]
        swapped = pltpu.roll(hv, ROWS_PER_EXPERT, 0)
        h_lo = jnp.where(low, hv, swapped)
        h_hi = jnp.where(low, swapped, hv)
        for p, lo, hi in _expert_pairs(tab_ref, idx_refs, t, low):
            prod_scr[p * SUBLANES:(p + 1) * SUBLANES, :] = lo * h_lo + hi * h_hi

    def dots(prod_scr):
        part = prod_scr[pl.ds(0, SLOTS, stride=ROWS_PER_EXPERT), :]
        for k in range(1, ROWS_PER_EXPERT):
            part = part + prod_scr[pl.ds(k, SLOTS, stride=ROWS_PER_EXPERT), :]
        return jnp.sum(part, axis=1, keepdims=True)

    def two_tokens(i, carry):
        t0 = 2 * i
        act_scr[...] = jnp.where(lane == t0 - 2, dots(prod_a),
                                 jnp.where(lane == t0 - 1, dots(prod_b), act_scr[...]))
        products(t0, prod_a)
        products(t0 + 1, prod_b)
        return carry

    lax.fori_loop(0, t_tile // 2, two_tokens, 0)
    act = jnp.where(lane == t_tile - 2, dots(prod_a), jnp.where(lane == t_tile - 1, dots(prod_b), act_scr[...]))
    gelu = 0.5 * act * (1.0 + lax.erf(act * (0.5 ** 0.5)))
    w_ref[0] = gate_ref[0] * gelu


def _head_idx_specs():
    return [pl.BlockSpec((PEER_TOPK * PEER_TILE,), lambda i: (i,), memory_space=pltpu.SMEM)] * PEER_HEADS


def _peer_act(head_idx, h, u_packed, gate):
    nt = gate.shape[0]
    t = PEER_TILE
    d = h.shape[1]
    prod = pltpu.VMEM((SLOTS * ROWS_PER_EXPERT, LANES), jnp.float32)
    return pl.pallas_call(
        _peer_act_kernel,
        grid=(nt,),
        in_specs=_head_idx_specs() + [
            pl.BlockSpec((t, d), lambda i: (i, 0)),
            pl.BlockSpec(u_packed.shape, lambda i: (0, 0), pipeline_mode=pl.Buffered(1)),
            pl.BlockSpec((1, SLOTS, t), lambda i: (i, 0, 0)),
        ],
        out_specs=pl.BlockSpec((1, SLOTS, t), lambda i: (i, 0, 0)),
        out_shape=jax.ShapeDtypeStruct((nt, SLOTS, t), jnp.float32),
        scratch_shapes=[pltpu.VMEM((t * SUBLANES, LANES), jnp.float32), prod, prod,
                        pltpu.VMEM((SLOTS, t), jnp.float32)],
        compiler_params=_params(1, V7X_VMEM_LIMIT_BYTES),
        name="peer_act",
    )(*head_idx, h, u_packed, gate)


def _peer_out_kernel(final_norm, *refs):
    idx_refs = refs[:PEER_HEADS]
    w_ref, x_ref, tab_ref, g_ref, o_ref, xrow_scr, wb_scr = refs[PEER_HEADS:]
    t_tile = PEER_TILE
    lane = lax.broadcasted_iota(jnp.int32, (SLOTS, t_tile), 1)
    low = lax.broadcasted_iota(jnp.int32, (SUBLANES, LANES), 0) < ROWS_PER_EXPERT
    _to_token_rows(x_ref, xrow_scr)

    def weight_rows(t):
        w_col = jnp.sum(jnp.where(lane == t, w_ref[0], 0.0), axis=1, keepdims=True)
        return jnp.broadcast_to(w_col, (SLOTS, LANES))

    def accumulate(t, wb_ref):
        acc_lo = jnp.zeros((SUBLANES, LANES), jnp.float32)
        acc_hi = jnp.zeros((SUBLANES, LANES), jnp.float32)
        for p, lo, hi in _expert_pairs(tab_ref, idx_refs, t, low):
            w2 = jnp.where(low, jnp.broadcast_to(wb_ref[2 * p:2 * p + 1, :], (SUBLANES, LANES)),
                           jnp.broadcast_to(wb_ref[2 * p + 1:2 * p + 2, :], (SUBLANES, LANES)))
            acc_lo = acc_lo + w2 * lo
            acc_hi = acc_hi + w2 * hi
        y = jnp.where(low, acc_lo + pltpu.roll(acc_lo, ROWS_PER_EXPERT, 0),
                      acc_hi + pltpu.roll(acc_hi, ROWS_PER_EXPERT, 0))
        rows = pl.ds(pl.multiple_of(t * SUBLANES, SUBLANES), SUBLANES)
        xrow_scr[rows, :] = xrow_scr[rows, :] + y

    def spread_weights(i, carry):
        for k in range(ACT_REDUCE_UNROLL):
            t = i * ACT_REDUCE_UNROLL + k
            wb_scr[t] = weight_rows(t)
        return carry

    lax.fori_loop(0, t_tile // ACT_REDUCE_UNROLL, spread_weights, 0)

    def two_tokens(i, carry):
        accumulate(2 * i, wb_scr.at[2 * i])
        accumulate(2 * i + 1, wb_scr.at[2 * i + 1])
        return carry

    lax.fori_loop(0, t_tile // 2, two_tokens, 0)
    for c in range(SUBLANES):
        o_ref[:, c * LANES:(c + 1) * LANES] = xrow_scr[pl.ds(c, t_tile, stride=SUBLANES), :]
    if final_norm:
        o_ref[...] = _rms(o_ref[...], g_ref[...])


def _peer_out(head_idx, w, x, v_packed, g_final):
    t = PEER_TILE
    n, d = x.shape
    nt = n // t
    g = jnp.ones((d,), jnp.float32) if g_final is None else g_final
    wb = pltpu.VMEM((t, SLOTS, LANES), jnp.float32)
    return pl.pallas_call(
        functools.partial(_peer_out_kernel, g_final is not None),
        grid=(nt,),
        in_specs=_head_idx_specs() + [
            pl.BlockSpec((1, SLOTS, t), lambda i: (i, 0, 0)),
            pl.BlockSpec((t, d), lambda i: (i, 0)),
            pl.BlockSpec(v_packed.shape, lambda i: (0, 0), pipeline_mode=pl.Buffered(1)),
            pl.BlockSpec((1, d), lambda i: (0, 0)),
        ],
        out_specs=pl.BlockSpec((t, d), lambda i: (i, 0)),
        out_shape=jax.ShapeDtypeStruct((n, d), jnp.float32),
        scratch_shapes=[pltpu.VMEM((t * SUBLANES, LANES), jnp.float32), wb],
        compiler_params=_params(1, V7X_VMEM_LIMIT_BYTES),
        name="peer_out",
    )(*head_idx, w, x, v_packed, g.reshape(1, d))


def peer_layer(x, g, w_query, sub_keys, u_packed, v_packed, g_final=None):
    n = x.shape[0]
    h, idx, gate = _peer_select(x, g, w_query.astype(jnp.bfloat16), sub_keys.astype(jnp.bfloat16))
    by_head = idx.reshape(n, PEER_HEADS, PEER_TOPK).transpose(1, 0, 2).reshape(PEER_HEADS, n * PEER_TOPK)
    head_idx = [by_head[hd] for hd in range(PEER_HEADS)]
    w = _peer_act(head_idx, h, u_packed, gate)
    return _peer_out(head_idx, w, x, v_packed, g_final)


def _trunk(x, seq_len, l0, l1, g_final):
    b = x.shape[0]
    x = x.reshape(b * seq_len, D_MODEL)
    x = conv_mixer_layer(x, l0["norm_mix"], l0["w_in"], l0["conv_w"], l0["w_out"], seq_len)
    x = peer_layer(x, l0["norm_ffn"], l0["w_query"], l0["sub_keys"], l0["u"], l0["v"])
    x = attn_mixer_layer(x, l1["norm_mix"], l1["w_qkv"], l1["w_o"], seq_len)
    x = peer_layer(x, l1["norm_ffn"], l1["w_query"], l1["sub_keys"], l1["u"], l1["v"], g_final)
    return x.reshape(b, seq_len, D_MODEL)


def kernel(x_prompt, x_sample, l0_norm_mix, l0_conv_w_in, l0_conv_w, l0_conv_w_out, l0_norm_ffn, l0_peer_w_query, l0_peer_sub_keys, l0_peer_u, l0_peer_v, l1_norm_mix, l1_attn_w_qkv, l1_attn_w_o, l1_norm_ffn, l1_peer_w_query, l1_peer_sub_keys, l1_peer_u, l1_peer_v, norm_final):
    l0 = dict(norm_mix=l0_norm_mix, w_in=l0_conv_w_in, conv_w=l0_conv_w, w_out=l0_conv_w_out,
              norm_ffn=l0_norm_ffn, w_query=l0_peer_w_query, sub_keys=l0_peer_sub_keys,
              u=_pack_table(l0_peer_u), v=_pack_table(l0_peer_v))
    l1 = dict(norm_mix=l1_norm_mix, w_qkv=l1_attn_w_qkv, w_o=l1_attn_w_o,
              norm_ffn=l1_norm_ffn, w_query=l1_peer_w_query, sub_keys=l1_peer_sub_keys,
              u=_pack_table(l1_peer_u), v=_pack_table(l1_peer_v))
    return (_trunk(x_prompt, x_prompt.shape[1], l0, l1, norm_final),
            _trunk(x_sample, x_sample.shape[1], l0, l1, norm_final))
```

```python
import functools

import jax
import jax.numpy as jnp
from jax import lax
from jax.experimental import pallas as pl
from jax.experimental.pallas import tpu as pltpu

D_MODEL = 1024
RMS_EPS = 1e-6
TOKEN_TILE = 512
V7X_VMEM_LIMIT_BYTES = 56 * 1024 * 1024


def _params(n_grid_dims, vmem=None):
    return pltpu.CompilerParams(
        dimension_semantics=("arbitrary",) * n_grid_dims,
        vmem_limit_bytes=vmem)


def _rms(x, g):
    return x * lax.rsqrt(jnp.mean(x * x, axis=-1, keepdims=True) + RMS_EPS) * g


def _conv_in_kernel(x_ref, g_ref, w_ref, b_ref, z_ref):
    h = _rms(x_ref[...], g_ref[...]).astype(jnp.bfloat16)
    y = jnp.dot(h, w_ref[...], preferred_element_type=jnp.float32)
    d = D_MODEL
    b_ref[...] = y[:, :d]
    z_ref[...] = y[:, d:2 * d] * y[:, 2 * d:]


def _conv_in(x, g, w_in_bf16):
    n, d = x.shape
    tm = TOKEN_TILE
    return pl.pallas_call(
        _conv_in_kernel,
        grid=(n // tm,),
        in_specs=[
            pl.BlockSpec((tm, d), lambda i: (i, 0)),
            pl.BlockSpec((1, d), lambda i: (0, 0)),
            pl.BlockSpec((d, 3 * d), lambda i: (0, 0)),
        ],
        out_specs=[pl.BlockSpec((tm, d), lambda i: (i, 0)),
                   pl.BlockSpec((tm, d), lambda i: (i, 0))],
        out_shape=[jax.ShapeDtypeStruct((n, d), jnp.float32)] * 2,
        compiler_params=_params(1, V7X_VMEM_LIMIT_BYTES),
        name="conv_in",
    )(x, g.reshape(1, d), w_in_bf16)


def _conv_out_kernel(tiles_per_seq, x_ref, b_ref, z_ref, zp_ref, zn_ref, cw_ref, w_ref, o_ref):
    i = pl.program_id(0)
    pos = i % tiles_per_seq
    z = z_ref[...]
    tm = z.shape[0]
    row = lax.broadcasted_iota(jnp.int32, z.shape, 0)
    prev_row = jnp.where(pos == 0, 0.0, zp_ref[7:8, :])
    next_row = jnp.where(pos == tiles_per_seq - 1, 0.0, zn_ref[0:1, :])
    z_prev = jnp.where(row == 0, prev_row, pltpu.roll(z, 1, 0))
    z_next = jnp.where(row == tm - 1, next_row, pltpu.roll(z, tm - 1, 0))
    cw = cw_ref[...]
    conv = z_prev * cw[0:1, :] + z * cw[1:2, :] + z_next * cw[2:3, :]
    a = (b_ref[...] * conv).astype(jnp.bfloat16)
    o_ref[...] = x_ref[...] + jnp.dot(a, w_ref[...], preferred_element_type=jnp.float32)


def _conv_out(x, b, z, conv_w, w_out_bf16, seq_len):
    n, d = x.shape
    tm = TOKEN_TILE
    tps = seq_len // tm
    sub = tm // 8
    nb8 = n // 8
    tile = lambda i: (i, 0)
    return pl.pallas_call(
        functools.partial(_conv_out_kernel, tps),
        grid=(n // tm,),
        in_specs=[
            pl.BlockSpec((tm, d), tile),
            pl.BlockSpec((tm, d), tile),
            pl.BlockSpec((tm, d), tile),
            pl.BlockSpec((8, d), lambda i: (jnp.maximum(i * sub - 1, 0), 0)),
            pl.BlockSpec((8, d), lambda i: (jnp.minimum((i + 1) * sub, nb8 - 1), 0)),
            pl.BlockSpec((3, d), lambda i: (0, 0)),
            pl.BlockSpec((d, d), lambda i: (0, 0)),
        ],
        out_specs=pl.BlockSpec((tm, d), tile),
        out_shape=jax.ShapeDtypeStruct((n, d), jnp.float32),
        compiler_params=_params(1, V7X_VMEM_LIMIT_BYTES),
        name="conv_out",
    )(x, b, z, z, z, conv_w, w_out_bf16)


def conv_mixer_layer(x, g, w_in, conv_w, w_out, seq_len):
    b, z = _conv_in(x, g, w_in.astype(jnp.bfloat16))
    return _conv_out(x, b, z, conv_w, w_out.astype(jnp.bfloat16), seq_len)


DILATED_CONFIGS = ((128, 1), (512, 4), (2048, 16))
N_GROUPS = len(DILATED_CONFIGS)
HEADS_PER_GROUP = 16
HEAD_DIM = D_MODEL // HEADS_PER_GROUP
ATTN_SIDE = 64
ATTN_Q_BLOCK = 128
LSE_LANES = 128


def _qkv_kernel(x_ref, g_ref, w_ref, *refs):
    out_refs, y_scr = refs[:N_GROUPS], refs[N_GROUPS]
    h = _rms(x_ref[...], g_ref[...]).astype(jnp.bfloat16)
    d = D_MODEL
    tm = x_ref.shape[0]
    for group, (_, dil) in enumerate(DILATED_CONFIGS):
        for c in range(3):
            col = (group * 3 + c) * d
            y = jnp.dot(h, w_ref[:, col:col + d], preferred_element_type=jnp.float32)
            if dil == 1:
                out_refs[group][0, :, c * d:(c + 1) * d] = y.astype(jnp.bfloat16)
                continue
            for k in range(d // LANES):
                y_scr[k] = y[:, k * LANES:(k + 1) * LANES]
            for r in range(dil):
                for k in range(d // LANES):
                    col = (r * 3 + c) * d + k * LANES
                    out_refs[group][0, :, col:col + LANES] = (
                        y_scr[k, pl.ds(r, tm // dil, stride=dil), :].astype(jnp.bfloat16))


def _qkv_proj(x, g, w_bf16, seq_len):
    n, d = x.shape
    tm = TOKEN_TILE
    tps = seq_len // tm
    batch = n // seq_len
    return pl.pallas_call(
        _qkv_kernel,
        grid=(n // tm,),
        in_specs=[
            pl.BlockSpec((tm, d), lambda i: (i, 0)),
            pl.BlockSpec((1, d), lambda i: (0, 0)),
            pl.BlockSpec(w_bf16.shape, lambda i: (0, 0), pipeline_mode=pl.Buffered(1)),
        ],
        out_specs=[pl.BlockSpec((1, tm // dil, dil * 3 * d), lambda i: (i // tps, i % tps, 0))
                   for _, dil in DILATED_CONFIGS],
        out_shape=[jax.ShapeDtypeStruct((batch, seq_len // dil, dil * 3 * d), jnp.bfloat16)
                   for _, dil in DILATED_CONFIGS],
        scratch_shapes=[pltpu.VMEM((d // LANES, tm, LANES), jnp.float32)],
        compiler_params=_params(1, V7X_VMEM_LIMIT_BYTES),
        name="qkv_proj",
    )(x, g.reshape(1, d), w_bf16)


def _alibi_slope(head):
    n = N_GROUPS * HEADS_PER_GROUP
    return 2.0 ** (-8.0 * (head + 1) / n)


def _attn_kernel(group, dilation, seq_strided, q_ref, kp_ref, kc_ref, kn_ref,
                 vp_ref, vc_ref, vn_ref, o_ref, lse_ref, k_scr, v_scr):
    j = pl.program_id(2)
    qb = ATTN_Q_BLOCK
    nk = qb + 2 * ATTN_SIDE
    for scr, (p_ref, c_ref, n_ref) in ((k_scr, (kp_ref, kc_ref, kn_ref)), (v_scr, (vp_ref, vc_ref, vn_ref))):
        scr[0:ATTN_SIDE] = p_ref[0]
        scr[ATTN_SIDE:ATTN_SIDE + qb] = c_ref[0]
        scr[ATTN_SIDE + qb:nk] = n_ref[0]
    qi = lax.broadcasted_iota(jnp.int32, (qb, nk), 0)
    ci = lax.broadcasted_iota(jnp.int32, (qb, nk), 1)
    rel = ci - ATTN_SIDE - qi
    key_pos = j * qb - ATTN_SIDE + ci
    valid = (jnp.abs(rel) <= ATTN_SIDE) & (key_pos >= 0) & (key_pos < seq_strided)
    dist = jnp.where(valid, (dilation * jnp.abs(rel)).astype(jnp.float32), jnp.inf)
    lane = lax.broadcasted_iota(jnp.int32, (qb, LSE_LANES), 1)
    first = lane < HEAD_DIM
    pair_mask = [first.astype(jnp.bfloat16), (~first).astype(jnp.bfloat16)]
    lse_all = jnp.zeros((qb, LSE_LANES), jnp.float32)
    for pair in range(HEADS_PER_GROUP // 2):
        cols = slice(pair * LANES, (pair + 1) * LANES)
        q2, k2, v2 = q_ref[0, :, cols], k_scr[:, cols], v_scr[:, cols]
        outs = []
        for half in range(2):
            h = 2 * pair + half
            s = lax.dot_general(q2 * pair_mask[half], k2, (((1,), (1,)), ((), ())),
                                preferred_element_type=jnp.float32)
            slope = _alibi_slope(group * HEADS_PER_GROUP + h)
            s = s * (HEAD_DIM ** -0.5) - slope * dist
            m = jnp.max(s, axis=-1, keepdims=True)
            p = jnp.exp(s - m)
            den = jnp.sum(p, axis=-1, keepdims=True)
            o = jnp.dot(p.astype(jnp.bfloat16), v2, preferred_element_type=jnp.float32)
            outs.append(o / den)
            lse_all = jnp.where(lane == h, m + jnp.log(den), lse_all)
        o_ref[0, :, cols] = jnp.where(first, outs[0], outs[1])
    lse_ref[0] = lse_all


def _group_attention(view, group, dilation):
    d = D_MODEL
    batch, ls = view.shape[0], view.shape[1]
    width = 3
    qb = ATTN_Q_BLOCK
    sb = qb // ATTN_SIDE
    n_side = ls // ATTN_SIDE
    col0 = 0

    def cur(c):
        return pl.BlockSpec((1, qb, d), lambda b, r, j: (b, j, r * width + col0 + c))

    def prev(c):
        return pl.BlockSpec((1, ATTN_SIDE, d),
                            lambda b, r, j: (b, jnp.maximum(j * sb - 1, 0), r * width + col0 + c))

    def nxt(c):
        return pl.BlockSpec((1, ATTN_SIDE, d),
                            lambda b, r, j: (b, jnp.minimum((j + 1) * sb, n_side - 1), r * width + col0 + c))

    return pl.pallas_call(
        functools.partial(_attn_kernel, group, dilation, ls),
        grid=(batch, dilation, ls // qb),
        in_specs=[cur(0), prev(1), cur(1), nxt(1), prev(2), cur(2), nxt(2)],
        out_specs=[pl.BlockSpec((1, qb, d), lambda b, r, j: (b, j, r)),
                   pl.BlockSpec((1, qb, LSE_LANES), lambda b, r, j: (b, j, r))],
        out_shape=[jax.ShapeDtypeStruct((batch, ls, dilation * d), jnp.float32),
                   jax.ShapeDtypeStruct((batch, ls, dilation * LSE_LANES), jnp.float32)],
        scratch_shapes=[pltpu.VMEM((qb + 2 * ATTN_SIDE, d), jnp.bfloat16)] * 2,
        compiler_params=_params(3, V7X_VMEM_LIMIT_BYTES),
        name=f"attn_g{group}",
    )(view, view, view, view, view, view, view)


def _token_order(ref, width, dil, scr):
    if dil == 1:
        return ref[0]
    rows = ref.shape[1]
    chunks = width // LANES
    for r in range(dil):
        for k in range(chunks):
            col = r * width + k * LANES
            scr[k, pl.ds(r, rows, stride=dil), :] = ref[0, :, col:col + LANES]
    return jnp.concatenate([scr[k] for k in range(chunks)], axis=1)


def _merge_kernel(x_ref, o0_ref, o1_ref, o2_ref, l0_ref, l1_ref, l2_ref, e_ref, w_ref, out_ref,
                  o_scr, l_scr):
    dils = [dil for _, dil in DILATED_CONFIGS]
    l0, l1, l2 = [_token_order(ref, LSE_LANES, dil, l_scr)
                  for ref, dil in zip((l0_ref, l1_ref, l2_ref), dils)]
    m = jnp.maximum(jnp.maximum(l0, l1), l2)
    e0, e1, e2 = jnp.exp(l0 - m), jnp.exp(l1 - m), jnp.exp(l2 - m)
    tot = e0 + e1 + e2
    expand = lambda a: jnp.dot(a, e_ref[...], preferred_element_type=jnp.float32,
                               precision=lax.Precision.HIGHEST)
    merged = jnp.zeros(x_ref.shape, jnp.float32)
    for e, o_ref, dil in zip((e0, e1, e2), (o0_ref, o1_ref, o2_ref), dils):
        merged = merged + expand(e / tot) * _token_order(o_ref, D_MODEL, dil, o_scr)
    out_ref[...] = x_ref[...] + jnp.dot(merged.astype(jnp.bfloat16), w_ref[...],
                                        preferred_element_type=jnp.float32)


def _attn_merge(x, outs, lses, w_o_bf16, seq_len):
    n, d = x.shape
    tm = TOKEN_TILE
    tps = seq_len // tm
    head_of_col = jnp.arange(d, dtype=jnp.int32) // HEAD_DIM
    expand = (jnp.arange(LSE_LANES, dtype=jnp.int32)[:, None] == head_of_col[None, :]).astype(jnp.float32)
    tile = lambda i: (i, 0)
    const = lambda i: (0, 0)
    strided = lambda i: (i // tps, i % tps, 0)
    dils = [dil for _, dil in DILATED_CONFIGS]
    return pl.pallas_call(
        _merge_kernel,
        grid=(n // tm,),
        in_specs=[pl.BlockSpec((tm, d), tile)]
        + [pl.BlockSpec((1, tm // dil, dil * d), strided) for dil in dils]
        + [pl.BlockSpec((1, tm // dil, dil * LSE_LANES), strided) for dil in dils]
        + [pl.BlockSpec((LSE_LANES, d), const), pl.BlockSpec((d, d), const)],
        out_specs=pl.BlockSpec((tm, d), tile),
        out_shape=jax.ShapeDtypeStruct((n, d), jnp.float32),
        scratch_shapes=[pltpu.VMEM((d // LANES, tm, LANES), jnp.float32),
                        pltpu.VMEM((1, tm, LSE_LANES), jnp.float32)],
        compiler_params=_params(1, V7X_VMEM_LIMIT_BYTES),
        name="attn_merge",
    )(x, *outs, *lses, expand, w_o_bf16)


def attn_mixer_layer(x, g, w_qkv, w_o, seq_len):
    views = _qkv_proj(x, g, w_qkv.astype(jnp.bfloat16), seq_len)
    outs, lses = [], []
    for group, (window, dilation) in enumerate(DILATED_CONFIGS):
        assert window // (2 * dilation) == ATTN_SIDE
        o, lse = _group_attention(views[group], group, dilation)
        outs.append(o)
        lses.append(lse)
    return _attn_merge(x, outs, lses, w_o.astype(jnp.bfloat16), seq_len)


PEER_HEADS = 8
PEER_HALF = 128
N_KEYS = 128
PEER_TOPK = 16
PEER_TILE = 128
PEER_SELECT_TILE = 256
ACT_REDUCE_UNROLL = 16
SLOTS = PEER_HEADS * PEER_TOPK
LANES = 128
HALF_D = D_MODEL // 2
ROWS_PER_EXPERT = HALF_D // LANES
TABLE_PAD_ROWS = 8 * ROWS_PER_EXPERT
SUBLANES = 8


def _take_top(s, ids, n):
    vals, picks = [], []
    for _ in range(n):
        m = jnp.max(s, axis=0, keepdims=True)
        sel = jnp.min(jnp.where(s == m, ids, jnp.int32(2 ** 30)), axis=0, keepdims=True)
        s = jnp.where(ids == sel, -jnp.inf, s)
        vals.append(m)
        picks.append(sel)
    return vals, picks


def _peer_select_kernel(x_ref, g_ref, wq_ref, keys_ref, h_ref, idx_ref, gate_ref, q_scr):
    hd = pl.program_id(1)
    t = x_ref.shape[0]

    @pl.when(hd == 0)
    def _():
        h = _rms(x_ref[...], g_ref[...])
        h_ref[...] = h
        q = jnp.dot(h.astype(jnp.bfloat16), wq_ref[...], preferred_element_type=jnp.float32)
        for k in range(PEER_HEADS):
            q_scr[k] = q[:, k * 2 * PEER_HALF:(k + 1) * 2 * PEER_HALF].astype(jnp.bfloat16)

    q = q_scr[hd]
    key_id = lax.broadcasted_iota(jnp.int32, (N_KEYS, t), 0)
    halves = []
    for p in range(2):
        s = lax.dot_general(keys_ref[0, p], q[:, p * PEER_HALF:(p + 1) * PEER_HALF],
                            (((1,), (1,)), ((), ())), preferred_element_type=jnp.float32)
        vals, ids = _take_top(s, key_id, PEER_TOPK)
        halves.append((jnp.concatenate(vals, axis=0), jnp.concatenate(ids, axis=0)))
    (s1, i1), (s2, i2) = halves

    n_experts = N_KEYS * N_KEYS
    cv, cid = [], []
    for b in range(SUBLANES):
        rows = PEER_TOPK if b == 0 else SUBLANES
        a_id = lax.broadcasted_iota(jnp.int32, (rows, t), 0)
        v = s1[0:rows] + s2[b:b + 1]
        na = PEER_TOPK // (b + 1)
        if na < rows:
            v = jnp.where(a_id < na, v, -jnp.inf)
        cv.append(v)
        cid.append((a_id * PEER_TOPK + b) * n_experts + i1[0:rows] * N_KEYS + i2[b:b + 1])
    b_id = lax.broadcasted_iota(jnp.int32, (SUBLANES, t), 0) + SUBLANES
    cv.append(s1[0:1] + s2[SUBLANES:])
    cid.append(b_id * n_experts + i1[0:1] * N_KEYS + i2[SUBLANES:])
    top, picks = _take_top(jnp.concatenate(cv, axis=0), jnp.concatenate(cid, axis=0), PEER_TOPK)
    top = jnp.concatenate(top, axis=0)
    e = jnp.exp(top - top[0:1])
    gate_ref[0] = e / jnp.sum(e, axis=0, keepdims=True)
    experts = jnp.concatenate(picks, axis=0) & (n_experts - 1)
    idx_ref[0] = experts * ROWS_PER_EXPERT + TABLE_PAD_ROWS


def _peer_select(x, g, wq_bf16, keys_bf16):
    n, d = x.shape
    t = PEER_SELECT_TILE
    nt = n // t
    return pl.pallas_call(
        _peer_select_kernel,
        grid=(nt, PEER_HEADS),
        in_specs=[
            pl.BlockSpec((t, d), lambda i, h: (i, 0)),
            pl.BlockSpec((1, d), lambda i, h: (0, 0)),
            pl.BlockSpec(wq_bf16.shape, lambda i, h: (0, 0)),
            pl.BlockSpec((1, 2, N_KEYS, PEER_HALF), lambda i, h: (h, 0, 0, 0)),
        ],
        out_specs=[pl.BlockSpec((t, d), lambda i, h: (i, 0)),
                   pl.BlockSpec((1, PEER_TOPK, t), lambda i, h: (h, i, 0)),
                   pl.BlockSpec((1, PEER_TOPK, t), lambda i, h: (i, h, 0))],
        out_shape=[jax.ShapeDtypeStruct((n, d), jnp.float32),
                   jax.ShapeDtypeStruct((PEER_HEADS, nt * PEER_TOPK, t), jnp.int32),
                   jax.ShapeDtypeStruct((nt, SLOTS, t), jnp.float32)],
        scratch_shapes=[pltpu.VMEM((PEER_HEADS, t, 2 * PEER_HALF), jnp.bfloat16)],
        compiler_params=_params(2, V7X_VMEM_LIMIT_BYTES),
        name="peer_select",
    )(x, g.reshape(1, d), wq_bf16, keys_bf16)


def _pack_table(tab):
    zero_experts = jnp.zeros((TABLE_PAD_ROWS // ROWS_PER_EXPERT, tab.shape[1]), tab.dtype)
    tab = jnp.concatenate([zero_experts, tab, zero_experts], axis=0)
    bits = lax.bitcast_convert_type(tab.astype(jnp.bfloat16), jnp.uint16).astype(jnp.uint32)
    packed = bits[:, :HALF_D] | (bits[:, HALF_D:] << 16)
    return packed.reshape(tab.shape[0] * ROWS_PER_EXPERT, LANES)


def _load_expert_pair(tab_ref, head_idx_ref, k, col, low_sublanes):
    ra = pl.multiple_of(head_idx_ref[k * PEER_SELECT_TILE + col], ROWS_PER_EXPERT)
    rb = pl.multiple_of(head_idx_ref[(k + 1) * PEER_SELECT_TILE + col] - ROWS_PER_EXPERT, ROWS_PER_EXPERT)
    pair = jnp.where(low_sublanes, tab_ref[pl.ds(ra, SUBLANES), :], tab_ref[pl.ds(rb, SUBLANES), :])
    lo = pltpu.bitcast(pair << 16, jnp.float32)
    hi = pltpu.bitcast(pair & jnp.uint32(0xFFFF0000), jnp.float32)
    return lo, hi


def _expert_pairs(tab_ref, idx_refs, t, low_sublanes):
    per_select = PEER_SELECT_TILE // PEER_TILE
    col = (pl.program_id(0) % per_select) * PEER_TILE + t
    for kk in range(PEER_TOPK // 2):
        for hd in range(PEER_HEADS):
            lo, hi = _load_expert_pair(tab_ref, idx_refs[hd], 2 * kk, col, low_sublanes)
            yield hd * (PEER_TOPK // 2) + kk, lo, hi


def _to_token_rows(src_ref, rows_scr):
    t = src_ref.shape[0]
    for c in range(SUBLANES):
        rows_scr[pl.ds(c, t, stride=SUBLANES), :] = src_ref[:, c * LANES:(c + 1) * LANES]


def _peer_act_kernel(*refs):
    idx_refs = refs[:PEER_HEADS]
    h_ref, tab_ref, gate_ref, w_ref, hrow_scr, prod_a, prod_b, act_scr = refs[PEER_HEADS:]
    t_tile = PEER_TILE
    lane = lax.broadcasted_iota(jnp.int32, (SLOTS, t_tile), 1)
    low = lax.broadcasted_iota(jnp.int32, (SUBLANES, LANES), 0) < ROWS_PER_EXPERT
    _to_token_rows(h_ref, hrow_scr)
    prod_a[...] = jnp.zeros(prod_a.shape, jnp.float32)
    prod_b[...] = jnp.zeros(prod_b.shape, jnp.float32)
    act_scr[...] = jnp.zeros(act_scr.shape, jnp.float32)

    def products(t, prod_scr):
        hv = hrow_scr[pl.ds(pl.multiple_of(t * SUBLANES, SUBLANES), SUBLANES), :]
        swapped = pltpu.roll(hv, ROWS_PER_EXPERT, 0)
        h_lo = jnp.where(low, hv, swapped)
        h_hi = jnp.where(low, swapped, hv)
        for p, lo, hi in _expert_pairs(tab_ref, idx_refs, t, low):
            prod_scr[p * SUBLANES:(p + 1) * SUBLANES, :] = lo * h_lo + hi * h_hi

    def dots(prod_scr):
        part = prod_scr[pl.ds(0, SLOTS, stride=ROWS_PER_EXPERT), :]
        for k in range(1, ROWS_PER_EXPERT):
            part = part + prod_scr[pl.ds(k, SLOTS, stride=ROWS_PER_EXPERT), :]
        return jnp.sum(part, axis=1, keepdims=True)

    def two_tokens(i, carry):
        t0 = 2 * i
        act_scr[...] = jnp.where(lane == t0 - 2, dots(prod_a),
                                 jnp.where(lane == t0 - 1, dots(prod_b), act_scr[...]))
        products(t0, prod_a)
        products(t0 + 1, prod_b)
        return carry

    lax.fori_loop(0, t_tile // 2, two_tokens, 0)
    act = jnp.where(lane == t_tile - 2, dots(prod_a), jnp.where(lane == t_tile - 1, dots(prod_b), act_scr[...]))
    gelu = 0.5 * act * (1.0 + lax.erf(act * (0.5 ** 0.5)))
    w_ref[0] = gate_ref[0] * gelu


def _head_idx_specs():
    per_select = PEER_SELECT_TILE // PEER_TILE
    return [pl.BlockSpec((PEER_TOPK * PEER_SELECT_TILE,), lambda i: (i // per_select,),
                         memory_space=pltpu.SMEM)] * PEER_HEADS


def _peer_act(head_idx, h, u_packed, gate):
    t = PEER_TILE
    n, d = h.shape
    nt = n // t
    per_gate = PEER_SELECT_TILE // PEER_TILE
    prod = pltpu.VMEM((SLOTS * ROWS_PER_EXPERT, LANES), jnp.float32)
    return pl.pallas_call(
        _peer_act_kernel,
        grid=(nt,),
        in_specs=_head_idx_specs() + [
            pl.BlockSpec((t, d), lambda i: (i, 0)),
            pl.BlockSpec(u_packed.shape, lambda i: (0, 0), pipeline_mode=pl.Buffered(1)),
            pl.BlockSpec((1, SLOTS, t), lambda i: (i // per_gate, 0, i % per_gate)),
        ],
        out_specs=pl.BlockSpec((1, SLOTS, t), lambda i: (i, 0, 0)),
        out_shape=jax.ShapeDtypeStruct((nt, SLOTS, t), jnp.float32),
        scratch_shapes=[pltpu.VMEM((t * SUBLANES, LANES), jnp.float32), prod, prod,
                        pltpu.VMEM((SLOTS, t), jnp.float32)],
        compiler_params=_params(1, V7X_VMEM_LIMIT_BYTES),
        name="peer_act",
    )(*head_idx, h, u_packed, gate)


def _peer_out_kernel(final_norm, *refs):
    idx_refs = refs[:PEER_HEADS]
    w_ref, x_ref, tab_ref, g_ref, o_ref, xrow_scr, wb_scr = refs[PEER_HEADS:]
    t_tile = PEER_TILE
    lane = lax.broadcasted_iota(jnp.int32, (SLOTS, t_tile), 1)
    low = lax.broadcasted_iota(jnp.int32, (SUBLANES, LANES), 0) < ROWS_PER_EXPERT
    _to_token_rows(x_ref, xrow_scr)

    def weight_rows(t):
        w_col = jnp.sum(jnp.where(lane == t, w_ref[0], 0.0), axis=1, keepdims=True)
        return jnp.broadcast_to(w_col, (SLOTS, LANES))

    def accumulate(t, wb_ref):
        acc_lo = jnp.zeros((SUBLANES, LANES), jnp.float32)
        acc_hi = jnp.zeros((SUBLANES, LANES), jnp.float32)
        for p, lo, hi in _expert_pairs(tab_ref, idx_refs, t, low):
            w2 = jnp.where(low, jnp.broadcast_to(wb_ref[2 * p:2 * p + 1, :], (SUBLANES, LANES)),
                           jnp.broadcast_to(wb_ref[2 * p + 1:2 * p + 2, :], (SUBLANES, LANES)))
            acc_lo = acc_lo + w2 * lo
            acc_hi = acc_hi + w2 * hi
        y = jnp.where(low, acc_lo + pltpu.roll(acc_lo, ROWS_PER_EXPERT, 0),
                      acc_hi + pltpu.roll(acc_hi, ROWS_PER_EXPERT, 0))
        rows = pl.ds(pl.multiple_of(t * SUBLANES, SUBLANES), SUBLANES)
        xrow_scr[rows, :] = xrow_scr[rows, :] + y

    def spread_weights(i, carry):
        for k in range(ACT_REDUCE_UNROLL):
            t = i * ACT_REDUCE_UNROLL + k
            wb_scr[t] = weight_rows(t)
        return carry

    lax.fori_loop(0, t_tile // ACT_REDUCE_UNROLL, spread_weights, 0)

    def two_tokens(i, carry):
        accumulate(2 * i, wb_scr.at[2 * i])
        accumulate(2 * i + 1, wb_scr.at[2 * i + 1])
        return carry

    lax.fori_loop(0, t_tile // 2, two_tokens, 0)
    for c in range(SUBLANES):
        o_ref[:, c * LANES:(c + 1) * LANES] = xrow_scr[pl.ds(c, t_tile, stride=SUBLANES), :]
    if final_norm:
        o_ref[...] = _rms(o_ref[...], g_ref[...])


def _peer_out(head_idx, w, x, v_packed, g_final):
    t = PEER_TILE
    n, d = x.shape
    nt = n // t
    g = jnp.ones((d,), jnp.float32) if g_final is None else g_final
    wb = pltpu.VMEM((t, SLOTS, LANES), jnp.float32)
    return pl.pallas_call(
        functools.partial(_peer_out_kernel, g_final is not None),
        grid=(nt,),
        in_specs=_head_idx_specs() + [
            pl.BlockSpec((1, SLOTS, t), lambda i: (i, 0, 0)),
            pl.BlockSpec((t, d), lambda i: (i, 0)),
            pl.BlockSpec(v_packed.shape, lambda i: (0, 0), pipeline_mode=pl.Buffered(1)),
            pl.BlockSpec((1, d), lambda i: (0, 0)),
        ],
        out_specs=pl.BlockSpec((t, d), lambda i: (i, 0)),
        out_shape=jax.ShapeDtypeStruct((n, d), jnp.float32),
        scratch_shapes=[pltpu.VMEM((t * SUBLANES, LANES), jnp.float32), wb],
        compiler_params=_params(1, V7X_VMEM_LIMIT_BYTES),
        name="peer_out",
    )(*head_idx, w, x, v_packed, g.reshape(1, d))


def peer_layer(x, g, w_query, sub_keys, u_packed, v_packed, g_final=None):
    h, idx, gate = _peer_select(x, g, w_query.astype(jnp.bfloat16), sub_keys.astype(jnp.bfloat16))
    flat = idx.reshape(PEER_HEADS, -1)
    head_idx = [flat[hd] for hd in range(PEER_HEADS)]
    w = _peer_act(head_idx, h, u_packed, gate)
    return _peer_out(head_idx, w, x, v_packed, g_final)


def _trunk(x, seq_len, l0, l1, g_final):
    b = x.shape[0]
    x = x.reshape(b * seq_len, D_MODEL)
    x = conv_mixer_layer(x, l0["norm_mix"], l0["w_in"], l0["conv_w"], l0["w_out"], seq_len)
    x = peer_layer(x, l0["norm_ffn"], l0["w_query"], l0["sub_keys"], l0["u"], l0["v"])
    x = attn_mixer_layer(x, l1["norm_mix"], l1["w_qkv"], l1["w_o"], seq_len)
    x = peer_layer(x, l1["norm_ffn"], l1["w_query"], l1["sub_keys"], l1["u"], l1["v"], g_final)
    return x.reshape(b, seq_len, D_MODEL)


def kernel(x_prompt, x_sample, l0_norm_mix, l0_conv_w_in, l0_conv_w, l0_conv_w_out, l0_norm_ffn, l0_peer_w_query, l0_peer_sub_keys, l0_peer_u, l0_peer_v, l1_norm_mix, l1_attn_w_qkv, l1_attn_w_o, l1_norm_ffn, l1_peer_w_query, l1_peer_sub_keys, l1_peer_u, l1_peer_v, norm_final):
    l0 = dict(norm_mix=l0_norm_mix, w_in=l0_conv_w_in, conv_w=l0_conv_w, w_out=l0_conv_w_out,
              norm_ffn=l0_norm_ffn, w_query=l0_peer_w_query, sub_keys=l0_peer_sub_keys,
              u=_pack_table(l0_peer_u), v=_pack_table(l0_peer_v))
    l1 = dict(norm_mix=l1_norm_mix, w_qkv=l1_attn_w_qkv, w_o=l1_attn_w_o,
              norm_ffn=l1_norm_ffn, w_query=l1_peer_w_query, sub_keys=l1_peer_sub_keys,
              u=_pack_table(l1_peer_u), v=_pack_table(l1_peer_v))
    return (_trunk(x_prompt, x_prompt.shape[1], l0, l1, norm_final),
            _trunk(x_sample, x_sample.shape[1], l0, l1, norm_final))
```

```python
import functools

import jax
import jax.numpy as jnp
from jax import lax
from jax.experimental import pallas as pl
from jax.experimental.pallas import tpu as pltpu

D_MODEL = 1024
RMS_EPS = 1e-6
TOKEN_TILE = 512
V7X_VMEM_LIMIT_BYTES = 56 * 1024 * 1024


def _params(n_grid_dims, vmem=None):
    return pltpu.CompilerParams(
        dimension_semantics=("arbitrary",) * n_grid_dims,
        vmem_limit_bytes=vmem)


def _rms(x, g):
    return x * lax.rsqrt(jnp.mean(x * x, axis=-1, keepdims=True) + RMS_EPS) * g


def _conv_in_kernel(x_ref, g_ref, w_ref, b_ref, z_ref):
    h = _rms(x_ref[...], g_ref[...]).astype(jnp.bfloat16)
    y = jnp.dot(h, w_ref[...], preferred_element_type=jnp.float32)
    d = D_MODEL
    b_ref[...] = y[:, :d]
    z_ref[...] = y[:, d:2 * d] * y[:, 2 * d:]


def _conv_in(x, g, w_in_bf16):
    n, d = x.shape
    tm = TOKEN_TILE
    return pl.pallas_call(
        _conv_in_kernel,
        grid=(n // tm,),
        in_specs=[
            pl.BlockSpec((tm, d), lambda i: (i, 0)),
            pl.BlockSpec((1, d), lambda i: (0, 0)),
            pl.BlockSpec((d, 3 * d), lambda i: (0, 0)),
        ],
        out_specs=[pl.BlockSpec((tm, d), lambda i: (i, 0)),
                   pl.BlockSpec((tm, d), lambda i: (i, 0))],
        out_shape=[jax.ShapeDtypeStruct((n, d), jnp.float32)] * 2,
        compiler_params=_params(1, V7X_VMEM_LIMIT_BYTES),
        name="conv_in",
    )(x, g.reshape(1, d), w_in_bf16)


def _conv_out_kernel(tiles_per_seq, x_ref, b_ref, z_ref, zp_ref, zn_ref, cw_ref, w_ref, o_ref):
    i = pl.program_id(0)
    pos = i % tiles_per_seq
    z = z_ref[...]
    tm = z.shape[0]
    row = lax.broadcasted_iota(jnp.int32, z.shape, 0)
    prev_row = jnp.where(pos == 0, 0.0, zp_ref[7:8, :])
    next_row = jnp.where(pos == tiles_per_seq - 1, 0.0, zn_ref[0:1, :])
    z_prev = jnp.where(row == 0, prev_row, pltpu.roll(z, 1, 0))
    z_next = jnp.where(row == tm - 1, next_row, pltpu.roll(z, tm - 1, 0))
    cw = cw_ref[...]
    conv = z_prev * cw[0:1, :] + z * cw[1:2, :] + z_next * cw[2:3, :]
    a = (b_ref[...] * conv).astype(jnp.bfloat16)
    o_ref[...] = x_ref[...] + jnp.dot(a, w_ref[...], preferred_element_type=jnp.float32)


def _conv_out(x, b, z, conv_w, w_out_bf16, seq_len):
    n, d = x.shape
    tm = TOKEN_TILE
    tps = seq_len // tm
    sub = tm // 8
    nb8 = n // 8
    tile = lambda i: (i, 0)
    return pl.pallas_call(
        functools.partial(_conv_out_kernel, tps),
        grid=(n // tm,),
        in_specs=[
            pl.BlockSpec((tm, d), tile),
            pl.BlockSpec((tm, d), tile),
            pl.BlockSpec((tm, d), tile),
            pl.BlockSpec((8, d), lambda i: (jnp.maximum(i * sub - 1, 0), 0)),
            pl.BlockSpec((8, d), lambda i: (jnp.minimum((i + 1) * sub, nb8 - 1), 0)),
            pl.BlockSpec((3, d), lambda i: (0, 0)),
            pl.BlockSpec((d, d), lambda i: (0, 0)),
        ],
        out_specs=pl.BlockSpec((tm, d), tile),
        out_shape=jax.ShapeDtypeStruct((n, d), jnp.float32),
        compiler_params=_params(1, V7X_VMEM_LIMIT_BYTES),
        name="conv_out",
    )(x, b, z, z, z, conv_w, w_out_bf16)


def conv_mixer_layer(x, g, w_in, conv_w, w_out, seq_len):
    b, z = _conv_in(x, g, w_in.astype(jnp.bfloat16))
    return _conv_out(x, b, z, conv_w, w_out.astype(jnp.bfloat16), seq_len)


DILATED_CONFIGS = ((128, 1), (512, 4), (2048, 16))
N_GROUPS = len(DILATED_CONFIGS)
HEADS_PER_GROUP = 16
HEAD_DIM = D_MODEL // HEADS_PER_GROUP
ATTN_SIDE = 64
ATTN_Q_BLOCK = 128
LSE_LANES = 128


def _qkv_kernel(x_ref, g_ref, w_ref, *refs):
    out_refs, y_scr = refs[:N_GROUPS], refs[N_GROUPS]
    h = _rms(x_ref[...], g_ref[...]).astype(jnp.bfloat16)
    d = D_MODEL
    tm = x_ref.shape[0]
    for group, (_, dil) in enumerate(DILATED_CONFIGS):
        for c in range(3):
            col = (group * 3 + c) * d
            y = jnp.dot(h, w_ref[:, col:col + d], preferred_element_type=jnp.float32)
            if dil == 1:
                out_refs[group][0, :, c * d:(c + 1) * d] = y.astype(jnp.bfloat16)
                continue
            for k in range(d // LANES):
                y_scr[k] = y[:, k * LANES:(k + 1) * LANES]
            for r in range(dil):
                for k in range(d // LANES):
                    col = (r * 3 + c) * d + k * LANES
                    out_refs[group][0, :, col:col + LANES] = (
                        y_scr[k, pl.ds(r, tm // dil, stride=dil), :].astype(jnp.bfloat16))


def _qkv_proj(x, g, w_bf16, seq_len):
    n, d = x.shape
    tm = TOKEN_TILE
    tps = seq_len // tm
    batch = n // seq_len
    return pl.pallas_call(
        _qkv_kernel,
        grid=(n // tm,),
        in_specs=[
            pl.BlockSpec((tm, d), lambda i: (i, 0)),
            pl.BlockSpec((1, d), lambda i: (0, 0)),
            pl.BlockSpec(w_bf16.shape, lambda i: (0, 0), pipeline_mode=pl.Buffered(1)),
        ],
        out_specs=[pl.BlockSpec((1, tm // dil, dil * 3 * d), lambda i: (i // tps, i % tps, 0))
                   for _, dil in DILATED_CONFIGS],
        out_shape=[jax.ShapeDtypeStruct((batch, seq_len // dil, dil * 3 * d), jnp.bfloat16)
                   for _, dil in DILATED_CONFIGS],
        scratch_shapes=[pltpu.VMEM((d // LANES, tm, LANES), jnp.float32)],
        compiler_params=_params(1, V7X_VMEM_LIMIT_BYTES),
        name="qkv_proj",
    )(x, g.reshape(1, d), w_bf16)


def _alibi_slope(head):
    n = N_GROUPS * HEADS_PER_GROUP
    return 2.0 ** (-8.0 * (head + 1) / n)


def _attn_kernel(group, dilation, seq_strided, q_ref, kp_ref, kc_ref, kn_ref,
                 vp_ref, vc_ref, vn_ref, o_ref, lse_ref, k_scr, v_scr):
    j = pl.program_id(2)
    qb = ATTN_Q_BLOCK
    nk = qb + 2 * ATTN_SIDE
    for scr, (p_ref, c_ref, n_ref) in ((k_scr, (kp_ref, kc_ref, kn_ref)), (v_scr, (vp_ref, vc_ref, vn_ref))):
        scr[0:ATTN_SIDE] = p_ref[0]
        scr[ATTN_SIDE:ATTN_SIDE + qb] = c_ref[0]
        scr[ATTN_SIDE + qb:nk] = n_ref[0]
    qi = lax.broadcasted_iota(jnp.int32, (qb, nk), 0)
    ci = lax.broadcasted_iota(jnp.int32, (qb, nk), 1)
    rel = ci - ATTN_SIDE - qi
    key_pos = j * qb - ATTN_SIDE + ci
    valid = (jnp.abs(rel) <= ATTN_SIDE) & (key_pos >= 0) & (key_pos < seq_strided)
    dist = jnp.where(valid, (dilation * jnp.abs(rel)).astype(jnp.float32), jnp.inf)
    lane = lax.broadcasted_iota(jnp.int32, (qb, LSE_LANES), 1)
    first = lane < HEAD_DIM
    pair_mask = [first.astype(jnp.bfloat16), (~first).astype(jnp.bfloat16)]
    lse_all = jnp.zeros((qb, LSE_LANES), jnp.float32)
    for pair in range(HEADS_PER_GROUP // 2):
        cols = slice(pair * LANES, (pair + 1) * LANES)
        q2, k2, v2 = q_ref[0, :, cols], k_scr[:, cols], v_scr[:, cols]
        outs = []
        for half in range(2):
            h = 2 * pair + half
            s = lax.dot_general(q2 * pair_mask[half], k2, (((1,), (1,)), ((), ())),
                                preferred_element_type=jnp.float32)
            slope = _alibi_slope(group * HEADS_PER_GROUP + h)
            s = s * (HEAD_DIM ** -0.5) - slope * dist
            m = jnp.max(s, axis=-1, keepdims=True)
            p = jnp.exp(s - m)
            den = jnp.sum(p, axis=-1, keepdims=True)
            o = jnp.dot(p.astype(jnp.bfloat16), v2, preferred_element_type=jnp.float32)
            outs.append(o / den)
            lse_all = jnp.where(lane == h, m + jnp.log(den), lse_all)
        o_ref[0, :, cols] = jnp.where(first, outs[0], outs[1])
    lse_ref[0] = lse_all


def _group_attention(view, group, dilation):
    d = D_MODEL
    batch, ls = view.shape[0], view.shape[1]
    width = 3
    qb = ATTN_Q_BLOCK
    sb = qb // ATTN_SIDE
    n_side = ls // ATTN_SIDE
    col0 = 0

    def cur(c):
        return pl.BlockSpec((1, qb, d), lambda b, r, j: (b, j, r * width + col0 + c))

    def prev(c):
        return pl.BlockSpec((1, ATTN_SIDE, d),
                            lambda b, r, j: (b, jnp.maximum(j * sb - 1, 0), r * width + col0 + c))

    def nxt(c):
        return pl.BlockSpec((1, ATTN_SIDE, d),
                            lambda b, r, j: (b, jnp.minimum((j + 1) * sb, n_side - 1), r * width + col0 + c))

    return pl.pallas_call(
        functools.partial(_attn_kernel, group, dilation, ls),
        grid=(batch, dilation, ls // qb),
        in_specs=[cur(0), prev(1), cur(1), nxt(1), prev(2), cur(2), nxt(2)],
        out_specs=[pl.BlockSpec((1, qb, d), lambda b, r, j: (b, j, r)),
                   pl.BlockSpec((1, qb, LSE_LANES), lambda b, r, j: (b, j, r))],
        out_shape=[jax.ShapeDtypeStruct((batch, ls, dilation * d), jnp.float32),
                   jax.ShapeDtypeStruct((batch, ls, dilation * LSE_LANES), jnp.float32)],
        scratch_shapes=[pltpu.VMEM((qb + 2 * ATTN_SIDE, d), jnp.bfloat16)] * 2,
        compiler_params=_params(3, V7X_VMEM_LIMIT_BYTES),
        name=f"attn_g{group}",
    )(view, view, view, view, view, view, view)


def _token_order(ref, width, dil, scr):
    if dil == 1:
        return ref[0]
    rows = ref.shape[1]
    chunks = width // LANES
    for r in range(dil):
        for k in range(chunks):
            col = r * width + k * LANES
            scr[k, pl.ds(r, rows, stride=dil), :] = ref[0, :, col:col + LANES]
    return jnp.concatenate([scr[k] for k in range(chunks)], axis=1)


def _merge_kernel(x_ref, o0_ref, o1_ref, o2_ref, l0_ref, l1_ref, l2_ref, e_ref, w_ref, out_ref,
                  o_scr, l_scr):
    dils = [dil for _, dil in DILATED_CONFIGS]
    l0, l1, l2 = [_token_order(ref, LSE_LANES, dil, l_scr)
                  for ref, dil in zip((l0_ref, l1_ref, l2_ref), dils)]
    m = jnp.maximum(jnp.maximum(l0, l1), l2)
    e0, e1, e2 = jnp.exp(l0 - m), jnp.exp(l1 - m), jnp.exp(l2 - m)
    tot = e0 + e1 + e2
    expand = lambda a: jnp.dot(a, e_ref[...], preferred_element_type=jnp.float32,
                               precision=lax.Precision.HIGHEST)
    merged = jnp.zeros(x_ref.shape, jnp.float32)
    for e, o_ref, dil in zip((e0, e1, e2), (o0_ref, o1_ref, o2_ref), dils):
        merged = merged + expand(e / tot) * _token_order(o_ref, D_MODEL, dil, o_scr)
    out_ref[...] = x_ref[...] + jnp.dot(merged.astype(jnp.bfloat16), w_ref[...],
                                        preferred_element_type=jnp.float32)


def _attn_merge(x, outs, lses, w_o_bf16, seq_len):
    n, d = x.shape
    tm = TOKEN_TILE
    tps = seq_len // tm
    head_of_col = jnp.arange(d, dtype=jnp.int32) // HEAD_DIM
    expand = (jnp.arange(LSE_LANES, dtype=jnp.int32)[:, None] == head_of_col[None, :]).astype(jnp.float32)
    tile = lambda i: (i, 0)
    const = lambda i: (0, 0)
    strided = lambda i: (i // tps, i % tps, 0)
    dils = [dil for _, dil in DILATED_CONFIGS]
    return pl.pallas_call(
        _merge_kernel,
        grid=(n // tm,),
        in_specs=[pl.BlockSpec((tm, d), tile)]
        + [pl.BlockSpec((1, tm // dil, dil * d), strided) for dil in dils]
        + [pl.BlockSpec((1, tm // dil, dil * LSE_LANES), strided) for dil in dils]
        + [pl.BlockSpec((LSE_LANES, d), const), pl.BlockSpec((d, d), const)],
        out_specs=pl.BlockSpec((tm, d), tile),
        out_shape=jax.ShapeDtypeStruct((n, d), jnp.float32),
        scratch_shapes=[pltpu.VMEM((d // LANES, tm, LANES), jnp.float32),
                        pltpu.VMEM((1, tm, LSE_LANES), jnp.float32)],
        compiler_params=_params(1, V7X_VMEM_LIMIT_BYTES),
        name="attn_merge",
    )(x, *outs, *lses, expand, w_o_bf16)


def attn_mixer_layer(x, g, w_qkv, w_o, seq_len):
    views = _qkv_proj(x, g, w_qkv.astype(jnp.bfloat16), seq_len)
    outs, lses = [], []
    for group, (window, dilation) in enumerate(DILATED_CONFIGS):
        assert window // (2 * dilation) == ATTN_SIDE
        o, lse = _group_attention(views[group], group, dilation)
        outs.append(o)
        lses.append(lse)
    return _attn_merge(x, outs, lses, w_o.astype(jnp.bfloat16), seq_len)


PEER_HEADS = 8
PEER_HALF = 128
N_KEYS = 128
PEER_TOPK = 16
PEER_TILE = 128
PEER_SELECT_TILE = 512
OUT_TOKENS_PER_STEP = 4
ACT_TOKENS_PER_STEP = 4
ACT_REDUCE_UNROLL = 16
SLOTS = PEER_HEADS * PEER_TOPK
LANES = 128
HALF_D = D_MODEL // 2
ROWS_PER_EXPERT = HALF_D // LANES
TABLE_PAD_ROWS = 8 * ROWS_PER_EXPERT
SUBLANES = 8


def _take_top(s, ids, n):
    vals, picks = [], []
    for _ in range(n):
        m = jnp.max(s, axis=0, keepdims=True)
        sel = jnp.min(jnp.where(s == m, ids, jnp.int32(2 ** 30)), axis=0, keepdims=True)
        s = jnp.where(ids == sel, -jnp.inf, s)
        vals.append(m)
        picks.append(sel)
    return vals, picks


def _peer_select_kernel(x_ref, g_ref, wq_ref, keys_ref, h_ref, idx_ref, gate_ref, q_scr):
    hd = pl.program_id(1)
    t = x_ref.shape[0]

    @pl.when(hd == 0)
    def _():
        h = _rms(x_ref[...], g_ref[...])
        h_ref[...] = h
        q = jnp.dot(h.astype(jnp.bfloat16), wq_ref[...], preferred_element_type=jnp.float32)
        for k in range(PEER_HEADS):
            q_scr[k] = q[:, k * 2 * PEER_HALF:(k + 1) * 2 * PEER_HALF].astype(jnp.bfloat16)

    q = q_scr[hd]
    key_id = lax.broadcasted_iota(jnp.int32, (N_KEYS, t), 0)
    halves = []
    for p in range(2):
        s = lax.dot_general(keys_ref[0, p], q[:, p * PEER_HALF:(p + 1) * PEER_HALF],
                            (((1,), (1,)), ((), ())), preferred_element_type=jnp.float32)
        vals, ids = _take_top(s, key_id, PEER_TOPK)
        halves.append((jnp.concatenate(vals, axis=0), jnp.concatenate(ids, axis=0)))
    (s1, i1), (s2, i2) = halves

    n_experts = N_KEYS * N_KEYS
    cv, cid = [], []
    for b in range(SUBLANES):
        rows = PEER_TOPK if b == 0 else SUBLANES
        a_id = lax.broadcasted_iota(jnp.int32, (rows, t), 0)
        v = s1[0:rows] + s2[b:b + 1]
        na = PEER_TOPK // (b + 1)
        if na < rows:
            v = jnp.where(a_id < na, v, -jnp.inf)
        cv.append(v)
        cid.append((a_id * PEER_TOPK + b) * n_experts + i1[0:rows] * N_KEYS + i2[b:b + 1])
    b_id = lax.broadcasted_iota(jnp.int32, (SUBLANES, t), 0) + SUBLANES
    cv.append(s1[0:1] + s2[SUBLANES:])
    cid.append(b_id * n_experts + i1[0:1] * N_KEYS + i2[SUBLANES:])
    top, picks = _take_top(jnp.concatenate(cv, axis=0), jnp.concatenate(cid, axis=0), PEER_TOPK)
    top = jnp.concatenate(top, axis=0)
    e = jnp.exp(top - top[0:1])
    gate_ref[0] = e / jnp.sum(e, axis=0, keepdims=True)
    experts = jnp.concatenate(picks, axis=0) & (n_experts - 1)
    idx_ref[0] = experts * ROWS_PER_EXPERT + TABLE_PAD_ROWS


def _peer_select(x, g, wq_bf16, keys_bf16):
    n, d = x.shape
    t = PEER_SELECT_TILE
    nt = n // t
    return pl.pallas_call(
        _peer_select_kernel,
        grid=(nt, PEER_HEADS),
        in_specs=[
            pl.BlockSpec((t, d), lambda i, h: (i, 0)),
            pl.BlockSpec((1, d), lambda i, h: (0, 0)),
            pl.BlockSpec(wq_bf16.shape, lambda i, h: (0, 0)),
            pl.BlockSpec((1, 2, N_KEYS, PEER_HALF), lambda i, h: (h, 0, 0, 0)),
        ],
        out_specs=[pl.BlockSpec((t, d), lambda i, h: (i, 0)),
                   pl.BlockSpec((1, PEER_TOPK, t), lambda i, h: (h, i, 0)),
                   pl.BlockSpec((1, PEER_TOPK, t), lambda i, h: (i, h, 0))],
        out_shape=[jax.ShapeDtypeStruct((n, d), jnp.float32),
                   jax.ShapeDtypeStruct((PEER_HEADS, nt * PEER_TOPK, t), jnp.int32),
                   jax.ShapeDtypeStruct((nt, SLOTS, t), jnp.float32)],
        scratch_shapes=[pltpu.VMEM((PEER_HEADS, t, 2 * PEER_HALF), jnp.bfloat16)],
        compiler_params=_params(2, V7X_VMEM_LIMIT_BYTES),
        name="peer_select",
    )(x, g.reshape(1, d), wq_bf16, keys_bf16)


def _pack_table(tab):
    zero_experts = jnp.zeros((TABLE_PAD_ROWS // ROWS_PER_EXPERT, tab.shape[1]), tab.dtype)
    tab = jnp.concatenate([zero_experts, tab, zero_experts], axis=0)
    bits = lax.bitcast_convert_type(tab.astype(jnp.bfloat16), jnp.uint16).astype(jnp.uint32)
    packed = bits[:, :HALF_D] | (bits[:, HALF_D:] << 16)
    return packed.reshape(tab.shape[0] * ROWS_PER_EXPERT, LANES)


def _load_expert_pair(tab_ref, head_idx_ref, k, col, low_sublanes):
    ra = pl.multiple_of(head_idx_ref[k * PEER_SELECT_TILE + col], ROWS_PER_EXPERT)
    rb = pl.multiple_of(head_idx_ref[(k + 1) * PEER_SELECT_TILE + col] - ROWS_PER_EXPERT, ROWS_PER_EXPERT)
    pair = jnp.where(low_sublanes, tab_ref[pl.ds(ra, SUBLANES), :], tab_ref[pl.ds(rb, SUBLANES), :])
    lo = pltpu.bitcast(pair << 16, jnp.float32)
    hi = pltpu.bitcast(pair & jnp.uint32(0xFFFF0000), jnp.float32)
    return lo, hi


def _expert_pairs(tab_ref, idx_refs, t, low_sublanes):
    per_select = PEER_SELECT_TILE // PEER_TILE
    col = (pl.program_id(0) % per_select) * PEER_TILE + t
    for kk in range(PEER_TOPK // 2):
        for hd in range(PEER_HEADS):
            lo, hi = _load_expert_pair(tab_ref, idx_refs[hd], 2 * kk, col, low_sublanes)
            yield hd * (PEER_TOPK // 2) + kk, lo, hi


def _to_token_rows(src_ref, rows_scr):
    t = src_ref.shape[0]
    for c in range(SUBLANES):
        rows_scr[pl.ds(c, t, stride=SUBLANES), :] = src_ref[:, c * LANES:(c + 1) * LANES]


def _peer_act_kernel(*refs):
    idx_refs = refs[:PEER_HEADS]
    h_ref, tab_ref, gate_ref, w_ref, hrow_scr, act_scr = refs[PEER_HEADS:PEER_HEADS + 6]
    prod_scrs = refs[PEER_HEADS + 6:]
    group = len(prod_scrs)
    t_tile = PEER_TILE
    lane = lax.broadcasted_iota(jnp.int32, (SLOTS, t_tile), 1)
    low = lax.broadcasted_iota(jnp.int32, (SUBLANES, LANES), 0) < ROWS_PER_EXPERT
    _to_token_rows(h_ref, hrow_scr)
    for prod_scr in prod_scrs:
        prod_scr[...] = jnp.zeros(prod_scr.shape, jnp.float32)
    act_scr[...] = jnp.zeros(act_scr.shape, jnp.float32)

    def products(t, prod_scr):
        hv = hrow_scr[pl.ds(pl.multiple_of(t * SUBLANES, SUBLANES), SUBLANES), :]
        swapped = pltpu.roll(hv, ROWS_PER_EXPERT, 0)
        h_lo = jnp.where(low, hv, swapped)
        h_hi = jnp.where(low, swapped, hv)
        for p, lo, hi in _expert_pairs(tab_ref, idx_refs, t, low):
            prod_scr[p * SUBLANES:(p + 1) * SUBLANES, :] = lo * h_lo + hi * h_hi

    def dots(prod_scr):
        part = prod_scr[pl.ds(0, SLOTS, stride=ROWS_PER_EXPERT), :]
        for k in range(1, ROWS_PER_EXPERT):
            part = part + prod_scr[pl.ds(k, SLOTS, stride=ROWS_PER_EXPERT), :]
        return jnp.sum(part, axis=1, keepdims=True)

    def with_dots(act, t0):
        for k, prod_scr in enumerate(prod_scrs):
            act = jnp.where(lane == t0 + k, dots(prod_scr), act)
        return act

    def token_group(i, carry):
        t0 = group * i
        act_scr[...] = with_dots(act_scr[...], t0 - group)
        for k, prod_scr in enumerate(prod_scrs):
            products(t0 + k, prod_scr)
        return carry

    lax.fori_loop(0, t_tile // group, token_group, 0)
    act = with_dots(act_scr[...], t_tile - group)
    gelu = 0.5 * act * (1.0 + lax.erf(act * (0.5 ** 0.5)))
    w_ref[0] = gate_ref[0] * gelu


def _head_idx_specs():
    per_select = PEER_SELECT_TILE // PEER_TILE
    return [pl.BlockSpec((PEER_TOPK * PEER_SELECT_TILE,), lambda i: (i // per_select,),
                         memory_space=pltpu.SMEM)] * PEER_HEADS


def _peer_act(head_idx, h, u_packed, gate):
    t = PEER_TILE
    n, d = h.shape
    nt = n // t
    per_gate = PEER_SELECT_TILE // PEER_TILE
    prod = pltpu.VMEM((SLOTS * ROWS_PER_EXPERT, LANES), jnp.float32)
    return pl.pallas_call(
        _peer_act_kernel,
        grid=(nt,),
        in_specs=_head_idx_specs() + [
            pl.BlockSpec((t, d), lambda i: (i, 0)),
            pl.BlockSpec(u_packed.shape, lambda i: (0, 0), pipeline_mode=pl.Buffered(1)),
            pl.BlockSpec((1, SLOTS, t), lambda i: (i // per_gate, 0, i % per_gate)),
        ],
        out_specs=pl.BlockSpec((1, SLOTS, t), lambda i: (i, 0, 0)),
        out_shape=jax.ShapeDtypeStruct((nt, SLOTS, t), jnp.float32),
        scratch_shapes=[pltpu.VMEM((t * SUBLANES, LANES), jnp.float32),
                        pltpu.VMEM((SLOTS, t), jnp.float32)] + [prod] * ACT_TOKENS_PER_STEP,
        compiler_params=_params(1, V7X_VMEM_LIMIT_BYTES),
        name="peer_act",
    )(*head_idx, h, u_packed, gate)


def _peer_out_kernel(final_norm, *refs):
    idx_refs = refs[:PEER_HEADS]
    w_ref, x_ref, tab_ref, g_ref, o_ref, xrow_scr, wb_scr = refs[PEER_HEADS:]
    t_tile = PEER_TILE
    lane = lax.broadcasted_iota(jnp.int32, (SLOTS, t_tile), 1)
    low = lax.broadcasted_iota(jnp.int32, (SUBLANES, LANES), 0) < ROWS_PER_EXPERT
    _to_token_rows(x_ref, xrow_scr)

    def weight_rows(t):
        w_col = jnp.sum(jnp.where(lane == t, w_ref[0], 0.0), axis=1, keepdims=True)
        return jnp.broadcast_to(w_col, (SLOTS, LANES))

    def accumulate(t, wb_ref):
        acc_lo = jnp.zeros((SUBLANES, LANES), jnp.float32)
        acc_hi = jnp.zeros((SUBLANES, LANES), jnp.float32)
        for p, lo, hi in _expert_pairs(tab_ref, idx_refs, t, low):
            w2 = jnp.where(low, jnp.broadcast_to(wb_ref[2 * p:2 * p + 1, :], (SUBLANES, LANES)),
                           jnp.broadcast_to(wb_ref[2 * p + 1:2 * p + 2, :], (SUBLANES, LANES)))
            acc_lo = acc_lo + w2 * lo
            acc_hi = acc_hi + w2 * hi
        y = jnp.where(low, acc_lo + pltpu.roll(acc_lo, ROWS_PER_EXPERT, 0),
                      acc_hi + pltpu.roll(acc_hi, ROWS_PER_EXPERT, 0))
        rows = pl.ds(pl.multiple_of(t * SUBLANES, SUBLANES), SUBLANES)
        xrow_scr[rows, :] = xrow_scr[rows, :] + y

    def spread_weights(i, carry):
        for k in range(ACT_REDUCE_UNROLL):
            t = i * ACT_REDUCE_UNROLL + k
            wb_scr[t] = weight_rows(t)
        return carry

    lax.fori_loop(0, t_tile // ACT_REDUCE_UNROLL, spread_weights, 0)

    def token_group(i, carry):
        for k in range(OUT_TOKENS_PER_STEP):
            t = i * OUT_TOKENS_PER_STEP + k
            accumulate(t, wb_scr.at[t])
        return carry

    lax.fori_loop(0, t_tile // OUT_TOKENS_PER_STEP, token_group, 0)
    for c in range(SUBLANES):
        o_ref[:, c * LANES:(c + 1) * LANES] = xrow_scr[pl.ds(c, t_tile, stride=SUBLANES), :]
    if final_norm:
        o_ref[...] = _rms(o_ref[...], g_ref[...])


def _peer_out(head_idx, w, x, v_packed, g_final):
    t = PEER_TILE
    n, d = x.shape
    nt = n // t
    g = jnp.ones((d,), jnp.float32) if g_final is None else g_final
    wb = pltpu.VMEM((t, SLOTS, LANES), jnp.float32)
    return pl.pallas_call(
        functools.partial(_peer_out_kernel, g_final is not None),
        grid=(nt,),
        in_specs=_head_idx_specs() + [
            pl.BlockSpec((1, SLOTS, t), lambda i: (i, 0, 0)),
            pl.BlockSpec((t, d), lambda i: (i, 0)),
            pl.BlockSpec(v_packed.shape, lambda i: (0, 0), pipeline_mode=pl.Buffered(1)),
            pl.BlockSpec((1, d), lambda i: (0, 0)),
        ],
        out_specs=pl.BlockSpec((t, d), lambda i: (i, 0)),
        out_shape=jax.ShapeDtypeStruct((n, d), jnp.float32),
        scratch_shapes=[pltpu.VMEM((t * SUBLANES, LANES), jnp.float32), wb],
        compiler_params=_params(1, V7X_VMEM_LIMIT_BYTES),
        name="peer_out",
    )(*head_idx, w, x, v_packed, g.reshape(1, d))


def peer_layer(x, g, w_query, sub_keys, u_packed, v_packed, g_final=None):
    h, idx, gate = _peer_select(x, g, w_query.astype(jnp.bfloat16), sub_keys.astype(jnp.bfloat16))
    flat = idx.reshape(PEER_HEADS, -1)
    head_idx = [flat[hd] for hd in range(PEER_HEADS)]
    w = _peer_act(head_idx, h, u_packed, gate)
    return _peer_out(head_idx, w, x, v_packed, g_final)


def _trunk(x, seq_len, l0, l1, g_final):
    b = x.shape[0]
    x = x.reshape(b * seq_len, D_MODEL)
    x = conv_mixer_layer(x, l0["norm_mix"], l0["w_in"], l0["conv_w"], l0["w_out"], seq_len)
    x = peer_layer(x, l0["norm_ffn"], l0["w_query"], l0["sub_keys"], l0["u"], l0["v"])
    x = attn_mixer_layer(x, l1["norm_mix"], l1["w_qkv"], l1["w_o"], seq_len)
    x = peer_layer(x, l1["norm_ffn"], l1["w_query"], l1["sub_keys"], l1["u"], l1["v"], g_final)
    return x.reshape(b, seq_len, D_MODEL)


def kernel(x_prompt, x_sample, l0_norm_mix, l0_conv_w_in, l0_conv_w, l0_conv_w_out, l0_norm_ffn, l0_peer_w_query, l0_peer_sub_keys, l0_peer_u, l0_peer_v, l1_norm_mix, l1_attn_w_qkv, l1_attn_w_o, l1_norm_ffn, l1_peer_w_query, l1_peer_sub_keys, l1_peer_u, l1_peer_v, norm_final):
    l0 = dict(norm_mix=l0_norm_mix, w_in=l0_conv_w_in, conv_w=l0_conv_w, w_out=l0_conv_w_out,
              norm_ffn=l0_norm_ffn, w_query=l0_peer_w_query, sub_keys=l0_peer_sub_keys,
              u=_pack_table(l0_peer_u), v=_pack_table(l0_peer_v))
    l1 = dict(norm_mix=l1_norm_mix, w_qkv=l1_attn_w_qkv, w_o=l1_attn_w_o,
              norm_ffn=l1_norm_ffn, w_query=l1_peer_w_query, sub_keys=l1_peer_sub_keys,
              u=_pack_table(l1_peer_u), v=_pack_table(l1_peer_v))
    return (_trunk(x_prompt, x_prompt.shape[1], l0, l1, norm_final),
            _trunk(x_sample, x_sample.shape[1], l0, l1, norm_final))
```

```python
import functools

import jax
import jax.numpy as jnp
from jax import lax
from jax.experimental import pallas as pl
from jax.experimental.pallas import tpu as pltpu

D_MODEL = 1024
RMS_EPS = 1e-6
TOKEN_TILE = 512
V7X_VMEM_LIMIT_BYTES = 56 * 1024 * 1024


def _params(n_grid_dims, vmem=None):
    return pltpu.CompilerParams(
        dimension_semantics=("arbitrary",) * n_grid_dims,
        vmem_limit_bytes=vmem)


def _rms(x, g):
    return x * lax.rsqrt(jnp.mean(x * x, axis=-1, keepdims=True) + RMS_EPS) * g


def _conv_in_kernel(x_ref, g_ref, w_ref, b_ref, z_ref):
    h = _rms(x_ref[...], g_ref[...]).astype(jnp.bfloat16)
    y = jnp.dot(h, w_ref[...], preferred_element_type=jnp.float32)
    d = D_MODEL
    b_ref[...] = y[:, :d]
    z_ref[...] = y[:, d:2 * d] * y[:, 2 * d:]


def _conv_in(x, g, w_in_bf16):
    n, d = x.shape
    tm = TOKEN_TILE
    return pl.pallas_call(
        _conv_in_kernel,
        grid=(n // tm,),
        in_specs=[
            pl.BlockSpec((tm, d), lambda i: (i, 0)),
            pl.BlockSpec((1, d), lambda i: (0, 0)),
            pl.BlockSpec((d, 3 * d), lambda i: (0, 0)),
        ],
        out_specs=[pl.BlockSpec((tm, d), lambda i: (i, 0)),
                   pl.BlockSpec((tm, d), lambda i: (i, 0))],
        out_shape=[jax.ShapeDtypeStruct((n, d), jnp.float32)] * 2,
        compiler_params=_params(1, V7X_VMEM_LIMIT_BYTES),
        name="conv_in",
    )(x, g.reshape(1, d), w_in_bf16)


def _conv_out_kernel(tiles_per_seq, x_ref, b_ref, z_ref, zp_ref, zn_ref, cw_ref, w_ref, o_ref):
    i = pl.program_id(0)
    pos = i % tiles_per_seq
    z = z_ref[...]
    tm = z.shape[0]
    row = lax.broadcasted_iota(jnp.int32, z.shape, 0)
    prev_row = jnp.where(pos == 0, 0.0, zp_ref[7:8, :])
    next_row = jnp.where(pos == tiles_per_seq - 1, 0.0, zn_ref[0:1, :])
    z_prev = jnp.where(row == 0, prev_row, pltpu.roll(z, 1, 0))
    z_next = jnp.where(row == tm - 1, next_row, pltpu.roll(z, tm - 1, 0))
    cw = cw_ref[...]
    conv = z_prev * cw[0:1, :] + z * cw[1:2, :] + z_next * cw[2:3, :]
    a = (b_ref[...] * conv).astype(jnp.bfloat16)
    o_ref[...] = x_ref[...] + jnp.dot(a, w_ref[...], preferred_element_type=jnp.float32)


def _conv_out(x, b, z, conv_w, w_out_bf16, seq_len):
    n, d = x.shape
    tm = TOKEN_TILE
    tps = seq_len // tm
    sub = tm // 8
    nb8 = n // 8
    tile = lambda i: (i, 0)
    return pl.pallas_call(
        functools.partial(_conv_out_kernel, tps),
        grid=(n // tm,),
        in_specs=[
            pl.BlockSpec((tm, d), tile),
            pl.BlockSpec((tm, d), tile),
            pl.BlockSpec((tm, d), tile),
            pl.BlockSpec((8, d), lambda i: (jnp.maximum(i * sub - 1, 0), 0)),
            pl.BlockSpec((8, d), lambda i: (jnp.minimum((i + 1) * sub, nb8 - 1), 0)),
            pl.BlockSpec((3, d), lambda i: (0, 0)),
            pl.BlockSpec((d, d), lambda i: (0, 0)),
        ],
        out_specs=pl.BlockSpec((tm, d), tile),
        out_shape=jax.ShapeDtypeStruct((n, d), jnp.float32),
        compiler_params=_params(1, V7X_VMEM_LIMIT_BYTES),
        name="conv_out",
    )(x, b, z, z, z, conv_w, w_out_bf16)


def conv_mixer_layer(x, g, w_in, conv_w, w_out, seq_len):
    b, z = _conv_in(x, g, w_in.astype(jnp.bfloat16))
    return _conv_out(x, b, z, conv_w, w_out.astype(jnp.bfloat16), seq_len)


DILATED_CONFIGS = ((128, 1), (512, 4), (2048, 16))
N_GROUPS = len(DILATED_CONFIGS)
HEADS_PER_GROUP = 16
HEAD_DIM = D_MODEL // HEADS_PER_GROUP
ATTN_SIDE = 64
ATTN_Q_BLOCK = 128
LSE_LANES = 128


def _qkv_kernel(x_ref, g_ref, w_ref, *refs):
    out_refs, y_scrs = refs[:N_GROUPS], refs[N_GROUPS:]
    h = _rms(x_ref[...], g_ref[...]).astype(jnp.bfloat16)
    d = D_MODEL
    tm = x_ref.shape[0]
    for group, (_, dil) in enumerate(DILATED_CONFIGS):
        for c in range(3):
            col = (group * 3 + c) * d
            y = jnp.dot(h, w_ref[:, col:col + d], preferred_element_type=jnp.float32)
            if dil == 1:
                out_refs[group][0, :, c * d:(c + 1) * d] = y.astype(jnp.bfloat16)
                continue
            y_scr = y_scrs[c % len(y_scrs)]
            for k in range(d // LANES):
                y_scr[k] = y[:, k * LANES:(k + 1) * LANES]
            for r in range(dil):
                for k in range(d // LANES):
                    col = (r * 3 + c) * d + k * LANES
                    out_refs[group][0, :, col:col + LANES] = (
                        y_scr[k, pl.ds(r, tm // dil, stride=dil), :].astype(jnp.bfloat16))


def _qkv_proj(x, g, w_bf16, seq_len):
    n, d = x.shape
    tm = TOKEN_TILE
    tps = seq_len // tm
    batch = n // seq_len
    return pl.pallas_call(
        _qkv_kernel,
        grid=(n // tm,),
        in_specs=[
            pl.BlockSpec((tm, d), lambda i: (i, 0)),
            pl.BlockSpec((1, d), lambda i: (0, 0)),
            pl.BlockSpec(w_bf16.shape, lambda i: (0, 0), pipeline_mode=pl.Buffered(1)),
        ],
        out_specs=[pl.BlockSpec((1, tm // dil, dil * 3 * d), lambda i: (i // tps, i % tps, 0))
                   for _, dil in DILATED_CONFIGS],
        out_shape=[jax.ShapeDtypeStruct((batch, seq_len // dil, dil * 3 * d), jnp.bfloat16)
                   for _, dil in DILATED_CONFIGS],
        scratch_shapes=[pltpu.VMEM((d // LANES, tm, LANES), jnp.float32)] * 2,
        compiler_params=_params(1, V7X_VMEM_LIMIT_BYTES),
        name="qkv_proj",
    )(x, g.reshape(1, d), w_bf16)


def _alibi_slope(head):
    n = N_GROUPS * HEADS_PER_GROUP
    return 2.0 ** (-8.0 * (head + 1) / n)


def _attn_kernel(group, dilation, seq_strided, q_ref, kp_ref, kc_ref, kn_ref,
                 vp_ref, vc_ref, vn_ref, o_ref, lse_ref, k_scr, v_scr):
    j = pl.program_id(2)
    qb = ATTN_Q_BLOCK
    nk = qb + 2 * ATTN_SIDE
    for scr, (p_ref, c_ref, n_ref) in ((k_scr, (kp_ref, kc_ref, kn_ref)), (v_scr, (vp_ref, vc_ref, vn_ref))):
        scr[0:ATTN_SIDE] = p_ref[0]
        scr[ATTN_SIDE:ATTN_SIDE + qb] = c_ref[0]
        scr[ATTN_SIDE + qb:nk] = n_ref[0]
    qi = lax.broadcasted_iota(jnp.int32, (qb, nk), 0)
    ci = lax.broadcasted_iota(jnp.int32, (qb, nk), 1)
    rel = ci - ATTN_SIDE - qi
    key_pos = j * qb - ATTN_SIDE + ci
    valid = (jnp.abs(rel) <= ATTN_SIDE) & (key_pos >= 0) & (key_pos < seq_strided)
    dist = jnp.where(valid, (dilation * jnp.abs(rel)).astype(jnp.float32), jnp.inf)
    lane = lax.broadcasted_iota(jnp.int32, (qb, LSE_LANES), 1)
    first = lane < HEAD_DIM
    pair_mask = [first.astype(jnp.bfloat16), (~first).astype(jnp.bfloat16)]
    lse_all = jnp.zeros((qb, LSE_LANES), jnp.float32)
    for pair in range(HEADS_PER_GROUP // 2):
        cols = slice(pair * LANES, (pair + 1) * LANES)
        q2, k2, v2 = q_ref[0, :, cols], k_scr[:, cols], v_scr[:, cols]
        outs = []
        for half in range(2):
            h = 2 * pair + half
            s = lax.dot_general(q2 * pair_mask[half], k2, (((1,), (1,)), ((), ())),
                                preferred_element_type=jnp.float32)
            slope = _alibi_slope(group * HEADS_PER_GROUP + h)
            s = s * (HEAD_DIM ** -0.5) - slope * dist
            m = jnp.max(s, axis=-1, keepdims=True)
            p = jnp.exp(s - m)
            den = jnp.sum(p, axis=-1, keepdims=True)
            o = jnp.dot(p.astype(jnp.bfloat16), v2, preferred_element_type=jnp.float32)
            outs.append(o / den)
            lse_all = jnp.where(lane == h, m + jnp.log(den), lse_all)
        o_ref[0, :, cols] = jnp.where(first, outs[0], outs[1])
    lse_ref[0] = lse_all


def _group_attention(view, group, dilation):
    d = D_MODEL
    batch, ls = view.shape[0], view.shape[1]
    width = 3
    qb = ATTN_Q_BLOCK
    sb = qb // ATTN_SIDE
    n_side = ls // ATTN_SIDE
    col0 = 0

    def cur(c):
        return pl.BlockSpec((1, qb, d), lambda b, r, j: (b, j, r * width + col0 + c))

    def prev(c):
        return pl.BlockSpec((1, ATTN_SIDE, d),
                            lambda b, r, j: (b, jnp.maximum(j * sb - 1, 0), r * width + col0 + c))

    def nxt(c):
        return pl.BlockSpec((1, ATTN_SIDE, d),
                            lambda b, r, j: (b, jnp.minimum((j + 1) * sb, n_side - 1), r * width + col0 + c))

    return pl.pallas_call(
        functools.partial(_attn_kernel, group, dilation, ls),
        grid=(batch, dilation, ls // qb),
        in_specs=[cur(0), prev(1), cur(1), nxt(1), prev(2), cur(2), nxt(2)],
        out_specs=[pl.BlockSpec((1, qb, d), lambda b, r, j: (b, j, r)),
                   pl.BlockSpec((1, qb, LSE_LANES), lambda b, r, j: (b, j, r))],
        out_shape=[jax.ShapeDtypeStruct((batch, ls, dilation * d), jnp.float32),
                   jax.ShapeDtypeStruct((batch, ls, dilation * LSE_LANES), jnp.float32)],
        scratch_shapes=[pltpu.VMEM((qb + 2 * ATTN_SIDE, d), jnp.bfloat16)] * 2,
        compiler_params=_params(3, V7X_VMEM_LIMIT_BYTES),
        name=f"attn_g{group}",
    )(view, view, view, view, view, view, view)


def _token_order(ref, width, dil, scr):
    if dil == 1:
        return ref[0]
    rows = ref.shape[1]
    chunks = width // LANES
    for r in range(dil):
        for k in range(chunks):
            col = r * width + k * LANES
            scr[k, pl.ds(r, rows, stride=dil), :] = ref[0, :, col:col + LANES]
    return jnp.concatenate([scr[k] for k in range(chunks)], axis=1)


def _merge_kernel(x_ref, o0_ref, o1_ref, o2_ref, l0_ref, l1_ref, l2_ref, e_ref, w_ref, out_ref,
                  o_scr, l_scr):
    dils = [dil for _, dil in DILATED_CONFIGS]
    l0, l1, l2 = [_token_order(ref, LSE_LANES, dil, l_scr)
                  for ref, dil in zip((l0_ref, l1_ref, l2_ref), dils)]
    m = jnp.maximum(jnp.maximum(l0, l1), l2)
    e0, e1, e2 = jnp.exp(l0 - m), jnp.exp(l1 - m), jnp.exp(l2 - m)
    tot = e0 + e1 + e2
    def expand(a):
        hi = a.astype(jnp.bfloat16)
        rest = a - hi.astype(jnp.float32)
        mid = rest.astype(jnp.bfloat16)
        lo = (rest - mid.astype(jnp.float32)).astype(jnp.bfloat16)
        return jnp.dot(jnp.concatenate([hi, mid, lo], axis=1), e_ref[...],
                       preferred_element_type=jnp.float32)
    merged = jnp.zeros(x_ref.shape, jnp.float32)
    for e, o_ref, dil in zip((e0, e1, e2), (o0_ref, o1_ref, o2_ref), dils):
        merged = merged + expand(e / tot) * _token_order(o_ref, D_MODEL, dil, o_scr)
    out_ref[...] = x_ref[...] + jnp.dot(merged.astype(jnp.bfloat16), w_ref[...],
                                        preferred_element_type=jnp.float32)


def _attn_merge(x, outs, lses, w_o_bf16, seq_len):
    n, d = x.shape
    tm = TOKEN_TILE
    tps = seq_len // tm
    head_of_col = jnp.arange(d, dtype=jnp.int32) // HEAD_DIM
    expand = (jnp.arange(LSE_LANES, dtype=jnp.int32)[:, None] == head_of_col[None, :]).astype(jnp.bfloat16)
    expand = jnp.concatenate([expand] * 3, axis=0)
    tile = lambda i: (i, 0)
    const = lambda i: (0, 0)
    strided = lambda i: (i // tps, i % tps, 0)
    dils = [dil for _, dil in DILATED_CONFIGS]
    return pl.pallas_call(
        _merge_kernel,
        grid=(n // tm,),
        in_specs=[pl.BlockSpec((tm, d), tile)]
        + [pl.BlockSpec((1, tm // dil, dil * d), strided) for dil in dils]
        + [pl.BlockSpec((1, tm // dil, dil * LSE_LANES), strided) for dil in dils]
        + [pl.BlockSpec((3 * LSE_LANES, d), const), pl.BlockSpec((d, d), const)],
        out_specs=pl.BlockSpec((tm, d), tile),
        out_shape=jax.ShapeDtypeStruct((n, d), jnp.float32),
        scratch_shapes=[pltpu.VMEM((d // LANES, tm, LANES), jnp.float32),
                        pltpu.VMEM((1, tm, LSE_LANES), jnp.float32)],
        compiler_params=_params(1, V7X_VMEM_LIMIT_BYTES),
        name="attn_merge",
    )(x, *outs, *lses, expand, w_o_bf16)


def attn_mixer_layer(x, g, w_qkv, w_o, seq_len):
    views = _qkv_proj(x, g, w_qkv.astype(jnp.bfloat16), seq_len)
    outs, lses = [], []
    for group, (window, dilation) in enumerate(DILATED_CONFIGS):
        assert window // (2 * dilation) == ATTN_SIDE
        o, lse = _group_attention(views[group], group, dilation)
        outs.append(o)
        lses.append(lse)
    return _attn_merge(x, outs, lses, w_o.astype(jnp.bfloat16), seq_len)


PEER_HEADS = 8
PEER_HALF = 128
N_KEYS = 128
PEER_TOPK = 16
PEER_TILE = 128
PEER_SELECT_TILE = 512
OUT_TOKENS_PER_STEP = 4
ACT_TOKENS_PER_STEP = 4
ACT_REDUCE_UNROLL = 16
SLOTS = PEER_HEADS * PEER_TOPK
LANES = 128
HALF_D = D_MODEL // 2
ROWS_PER_EXPERT = HALF_D // LANES
TABLE_PAD_ROWS = 8 * ROWS_PER_EXPERT
SUBLANES = 8


def _take_top(s, ids, n):
    vals, picks = [], []
    for _ in range(n):
        m = jnp.max(s, axis=0, keepdims=True)
        sel = jnp.min(jnp.where(s == m, ids, jnp.int32(2 ** 30)), axis=0, keepdims=True)
        s = jnp.where(ids == sel, -jnp.inf, s)
        vals.append(m)
        picks.append(sel)
    return vals, picks


def _peer_select_kernel(x_ref, g_ref, wq_ref, keys_ref, h_ref, idx_ref, gate_ref, q_scr):
    hd = pl.program_id(1)
    t = x_ref.shape[0]

    @pl.when(hd == 0)
    def _():
        h = _rms(x_ref[...], g_ref[...])
        h_ref[...] = h
        q = jnp.dot(h.astype(jnp.bfloat16), wq_ref[...], preferred_element_type=jnp.float32)
        for k in range(PEER_HEADS):
            q_scr[k] = q[:, k * 2 * PEER_HALF:(k + 1) * 2 * PEER_HALF].astype(jnp.bfloat16)

    q = q_scr[hd]
    key_id = lax.broadcasted_iota(jnp.int32, (N_KEYS, t), 0)
    halves = []
    for p in range(2):
        s = lax.dot_general(keys_ref[0, p], q[:, p * PEER_HALF:(p + 1) * PEER_HALF],
                            (((1,), (1,)), ((), ())), preferred_element_type=jnp.float32)
        vals, ids = _take_top(s, key_id, PEER_TOPK)
        halves.append((jnp.concatenate(vals, axis=0), jnp.concatenate(ids, axis=0)))
    (s1, i1), (s2, i2) = halves

    n_experts = N_KEYS * N_KEYS
    cv, cid = [], []
    for b in range(SUBLANES):
        rows = PEER_TOPK if b == 0 else SUBLANES
        a_id = lax.broadcasted_iota(jnp.int32, (rows, t), 0)
        v = s1[0:rows] + s2[b:b + 1]
        na = PEER_TOPK // (b + 1)
        if na < rows:
            v = jnp.where(a_id < na, v, -jnp.inf)
        cv.append(v)
        cid.append((a_id * PEER_TOPK + b) * n_experts + i1[0:rows] * N_KEYS + i2[b:b + 1])
    b_id = lax.broadcasted_iota(jnp.int32, (SUBLANES, t), 0) + SUBLANES
    cv.append(s1[0:1] + s2[SUBLANES:])
    cid.append(b_id * n_experts + i1[0:1] * N_KEYS + i2[SUBLANES:])
    top, picks = _take_top(jnp.concatenate(cv, axis=0), jnp.concatenate(cid, axis=0), PEER_TOPK)
    top = jnp.concatenate(top, axis=0)
    e = jnp.exp(top - top[0:1])
    gate_ref[0] = e / jnp.sum(e, axis=0, keepdims=True)
    experts = jnp.concatenate(picks, axis=0) & (n_experts - 1)
    idx_ref[0] = experts * ROWS_PER_EXPERT + TABLE_PAD_ROWS


def _peer_select(x, g, wq_bf16, keys_bf16):
    n, d = x.shape
    t = PEER_SELECT_TILE
    nt = n // t
    return pl.pallas_call(
        _peer_select_kernel,
        grid=(nt, PEER_HEADS),
        in_specs=[
            pl.BlockSpec((t, d), lambda i, h: (i, 0)),
            pl.BlockSpec((1, d), lambda i, h: (0, 0)),
            pl.BlockSpec(wq_bf16.shape, lambda i, h: (0, 0)),
            pl.BlockSpec((1, 2, N_KEYS, PEER_HALF), lambda i, h: (h, 0, 0, 0)),
        ],
        out_specs=[pl.BlockSpec((t, d), lambda i, h: (i, 0)),
                   pl.BlockSpec((1, PEER_TOPK, t), lambda i, h: (h, i, 0)),
                   pl.BlockSpec((1, PEER_TOPK, t), lambda i, h: (i, h, 0))],
        out_shape=[jax.ShapeDtypeStruct((n, d), jnp.float32),
                   jax.ShapeDtypeStruct((PEER_HEADS, nt * PEER_TOPK, t), jnp.int32),
                   jax.ShapeDtypeStruct((nt, SLOTS, t), jnp.float32)],
        scratch_shapes=[pltpu.VMEM((PEER_HEADS, t, 2 * PEER_HALF), jnp.bfloat16)],
        compiler_params=_params(2, V7X_VMEM_LIMIT_BYTES),
        name="peer_select",
    )(x, g.reshape(1, d), wq_bf16, keys_bf16)


def _pack_table(tab):
    zero_experts = jnp.zeros((TABLE_PAD_ROWS // ROWS_PER_EXPERT, tab.shape[1]), tab.dtype)
    tab = jnp.concatenate([zero_experts, tab, zero_experts], axis=0)
    bits = lax.bitcast_convert_type(tab.astype(jnp.bfloat16), jnp.uint16).astype(jnp.uint32)
    packed = bits[:, :HALF_D] | (bits[:, HALF_D:] << 16)
    return packed.reshape(tab.shape[0] * ROWS_PER_EXPERT, LANES)


def _load_expert_pair(tab_ref, head_idx_ref, k, col, low_sublanes):
    ra = pl.multiple_of(head_idx_ref[k * PEER_SELECT_TILE + col], ROWS_PER_EXPERT)
    rb = pl.multiple_of(head_idx_ref[(k + 1) * PEER_SELECT_TILE + col] - ROWS_PER_EXPERT, ROWS_PER_EXPERT)
    pair = jnp.where(low_sublanes, tab_ref[pl.ds(ra, SUBLANES), :], tab_ref[pl.ds(rb, SUBLANES), :])
    lo = pltpu.bitcast(pair << 16, jnp.float32)
    hi = pltpu.bitcast(pair & jnp.uint32(0xFFFF0000), jnp.float32)
    return lo, hi


def _expert_pairs(tab_ref, idx_refs, t, low_sublanes):
    per_select = PEER_SELECT_TILE // PEER_TILE
    col = (pl.program_id(0) % per_select) * PEER_TILE + t
    for kk in range(PEER_TOPK // 2):
        for hd in range(PEER_HEADS):
            lo, hi = _load_expert_pair(tab_ref, idx_refs[hd], 2 * kk, col, low_sublanes)
            yield hd * (PEER_TOPK // 2) + kk, lo, hi


def _to_token_rows(src_ref, rows_scr):
    t = src_ref.shape[0]
    for c in range(SUBLANES):
        rows_scr[pl.ds(c, t, stride=SUBLANES), :] = src_ref[:, c * LANES:(c + 1) * LANES]


def _peer_act_kernel(*refs):
    idx_refs = refs[:PEER_HEADS]
    h_ref, tab_ref, gate_ref, w_ref, hrow_scr, act_scr = refs[PEER_HEADS:PEER_HEADS + 6]
    prod_scrs = refs[PEER_HEADS + 6:]
    group = len(prod_scrs)
    t_tile = PEER_TILE
    lane = lax.broadcasted_iota(jnp.int32, (SLOTS, t_tile), 1)
    low = lax.broadcasted_iota(jnp.int32, (SUBLANES, LANES), 0) < ROWS_PER_EXPERT
    _to_token_rows(h_ref, hrow_scr)
    for prod_scr in prod_scrs:
        prod_scr[...] = jnp.zeros(prod_scr.shape, jnp.float32)
    act_scr[...] = jnp.zeros(act_scr.shape, jnp.float32)

    def products(t, prod_scr):
        hv = hrow_scr[pl.ds(pl.multiple_of(t * SUBLANES, SUBLANES), SUBLANES), :]
        swapped = pltpu.roll(hv, ROWS_PER_EXPERT, 0)
        h_lo = jnp.where(low, hv, swapped)
        h_hi = jnp.where(low, swapped, hv)
        for p, lo, hi in _expert_pairs(tab_ref, idx_refs, t, low):
            prod_scr[p * SUBLANES:(p + 1) * SUBLANES, :] = lo * h_lo + hi * h_hi

    def dots(prod_scr):
        part = prod_scr[pl.ds(0, SLOTS, stride=ROWS_PER_EXPERT), :]
        for k in range(1, ROWS_PER_EXPERT):
            part = part + prod_scr[pl.ds(k, SLOTS, stride=ROWS_PER_EXPERT), :]
        return jnp.sum(part, axis=1, keepdims=True)

    def with_dots(act, t0):
        for k, prod_scr in enumerate(prod_scrs):
            act = jnp.where(lane == t0 + k, dots(prod_scr), act)
        return act

    def token_group(i, carry):
        t0 = group * i
        act_scr[...] = with_dots(act_scr[...], t0 - group)
        for k, prod_scr in enumerate(prod_scrs):
            products(t0 + k, prod_scr)
        return carry

    lax.fori_loop(0, t_tile // group, token_group, 0)
    act = with_dots(act_scr[...], t_tile - group)
    gelu = 0.5 * act * (1.0 + lax.erf(act * (0.5 ** 0.5)))
    w_ref[0] = gate_ref[0] * gelu


def _head_idx_specs():
    per_select = PEER_SELECT_TILE // PEER_TILE
    return [pl.BlockSpec((PEER_TOPK * PEER_SELECT_TILE,), lambda i: (i // per_select,),
                         memory_space=pltpu.SMEM)] * PEER_HEADS


def _peer_act(head_idx, h, u_packed, gate):
    t = PEER_TILE
    n, d = h.shape
    nt = n // t
    per_gate = PEER_SELECT_TILE // PEER_TILE
    prod = pltpu.VMEM((SLOTS * ROWS_PER_EXPERT, LANES), jnp.float32)
    return pl.pallas_call(
        _peer_act_kernel,
        grid=(nt,),
        in_specs=_head_idx_specs() + [
            pl.BlockSpec((t, d), lambda i: (i, 0)),
            pl.BlockSpec(u_packed.shape, lambda i: (0, 0), pipeline_mode=pl.Buffered(1)),
            pl.BlockSpec((1, SLOTS, t), lambda i: (i // per_gate, 0, i % per_gate)),
        ],
        out_specs=pl.BlockSpec((1, SLOTS, t), lambda i: (i, 0, 0)),
        out_shape=jax.ShapeDtypeStruct((nt, SLOTS, t), jnp.float32),
        scratch_shapes=[pltpu.VMEM((t * SUBLANES, LANES), jnp.float32),
                        pltpu.VMEM((SLOTS, t), jnp.float32)] + [prod] * ACT_TOKENS_PER_STEP,
        compiler_params=_params(1, V7X_VMEM_LIMIT_BYTES),
        name="peer_act",
    )(*head_idx, h, u_packed, gate)


def _peer_out_kernel(final_norm, *refs):
    idx_refs = refs[:PEER_HEADS]
    w_ref, w_next_ref, x_ref, tab_ref, g_ref, o_ref, xrow_scr, wb_even, wb_odd = refs[PEER_HEADS:]
    step = pl.program_id(0)
    t_tile = PEER_TILE
    lane = lax.broadcasted_iota(jnp.int32, (SLOTS, t_tile), 1)
    low = lax.broadcasted_iota(jnp.int32, (SUBLANES, LANES), 0) < ROWS_PER_EXPERT
    _to_token_rows(x_ref, xrow_scr)

    def weight_rows(t, tile_w_ref):
        w_col = jnp.sum(jnp.where(lane == t, tile_w_ref[0], 0.0), axis=1, keepdims=True)
        return jnp.broadcast_to(w_col, (SLOTS, LANES))

    def accumulate(t, wb_ref):
        acc_lo = jnp.zeros((SUBLANES, LANES), jnp.float32)
        acc_hi = jnp.zeros((SUBLANES, LANES), jnp.float32)
        for p, lo, hi in _expert_pairs(tab_ref, idx_refs, t, low):
            w2 = jnp.where(low, jnp.broadcast_to(wb_ref[2 * p:2 * p + 1, :], (SUBLANES, LANES)),
                           jnp.broadcast_to(wb_ref[2 * p + 1:2 * p + 2, :], (SUBLANES, LANES)))
            acc_lo = acc_lo + w2 * lo
            acc_hi = acc_hi + w2 * hi
        y = jnp.where(low, acc_lo + pltpu.roll(acc_lo, ROWS_PER_EXPERT, 0),
                      acc_hi + pltpu.roll(acc_hi, ROWS_PER_EXPERT, 0))
        rows = pl.ds(pl.multiple_of(t * SUBLANES, SUBLANES), SUBLANES)
        xrow_scr[rows, :] = xrow_scr[rows, :] + y

    @pl.when(step == 0)
    def _():
        def spread_weights(i, carry):
            for k in range(ACT_REDUCE_UNROLL):
                t = i * ACT_REDUCE_UNROLL + k
                wb_even[t] = weight_rows(t, w_ref)
            return carry

        lax.fori_loop(0, t_tile // ACT_REDUCE_UNROLL, spread_weights, 0)

    def run(wb_cur, wb_next):
        def token_group(i, carry):
            for k in range(OUT_TOKENS_PER_STEP):
                t = i * OUT_TOKENS_PER_STEP + k
                accumulate(t, wb_cur.at[t])
                wb_next[t] = weight_rows(t, w_next_ref)
            return carry

        lax.fori_loop(0, t_tile // OUT_TOKENS_PER_STEP, token_group, 0)

    @pl.when(step % 2 == 0)
    def _():
        run(wb_even, wb_odd)

    @pl.when(step % 2 == 1)
    def _():
        run(wb_odd, wb_even)

    for c in range(SUBLANES):
        o_ref[:, c * LANES:(c + 1) * LANES] = xrow_scr[pl.ds(c, t_tile, stride=SUBLANES), :]
    if final_norm:
        o_ref[...] = _rms(o_ref[...], g_ref[...])


def _peer_out(head_idx, w, x, v_packed, g_final):
    t = PEER_TILE
    n, d = x.shape
    nt = n // t
    g = jnp.ones((d,), jnp.float32) if g_final is None else g_final
    wb = pltpu.VMEM((t, SLOTS, LANES), jnp.float32)
    return pl.pallas_call(
        functools.partial(_peer_out_kernel, g_final is not None),
        grid=(nt,),
        in_specs=_head_idx_specs() + [
            pl.BlockSpec((1, SLOTS, t), lambda i: (i, 0, 0)),
            pl.BlockSpec((1, SLOTS, t), lambda i: (jnp.minimum(i + 1, nt - 1), 0, 0)),
            pl.BlockSpec((t, d), lambda i: (i, 0)),
            pl.BlockSpec(v_packed.shape, lambda i: (0, 0), pipeline_mode=pl.Buffered(1)),
            pl.BlockSpec((1, d), lambda i: (0, 0)),
        ],
        out_specs=pl.BlockSpec((t, d), lambda i: (i, 0)),
        out_shape=jax.ShapeDtypeStruct((n, d), jnp.float32),
        scratch_shapes=[pltpu.VMEM((t * SUBLANES, LANES), jnp.float32), wb, wb],
        compiler_params=_params(1, V7X_VMEM_LIMIT_BYTES),
        name="peer_out",
    )(*head_idx, w, w, x, v_packed, g.reshape(1, d))


def peer_layer(x, g, w_query, sub_keys, u_packed, v_packed, g_final=None):
    h, idx, gate = _peer_select(x, g, w_query.astype(jnp.bfloat16), sub_keys.astype(jnp.bfloat16))
    flat = idx.reshape(PEER_HEADS, -1)
    head_idx = [flat[hd] for hd in range(PEER_HEADS)]
    w = _peer_act(head_idx, h, u_packed, gate)
    return _peer_out(head_idx, w, x, v_packed, g_final)


def _trunk(x, seq_len, l0, l1, g_final):
    b = x.shape[0]
    x = x.reshape(b * seq_len, D_MODEL)
    x = conv_mixer_layer(x, l0["norm_mix"], l0["w_in"], l0["conv_w"], l0["w_out"], seq_len)
    x = peer_layer(x, l0["norm_ffn"], l0["w_query"], l0["sub_keys"], l0["u"], l0["v"])
    x = attn_mixer_layer(x, l1["norm_mix"], l1["w_qkv"], l1["w_o"], seq_len)
    x = peer_layer(x, l1["norm_ffn"], l1["w_query"], l1["sub_keys"], l1["u"], l1["v"], g_final)
    return x.reshape(b, seq_len, D_MODEL)


def kernel(x_prompt, x_sample, l0_norm_mix, l0_conv_w_in, l0_conv_w, l0_conv_w_out, l0_norm_ffn, l0_peer_w_query, l0_peer_sub_keys, l0_peer_u, l0_peer_v, l1_norm_mix, l1_attn_w_qkv, l1_attn_w_o, l1_norm_ffn, l1_peer_w_query, l1_peer_sub_keys, l1_peer_u, l1_peer_v, norm_final):
    l0 = dict(norm_mix=l0_norm_mix, w_in=l0_conv_w_in, conv_w=l0_conv_w, w_out=l0_conv_w_out,
              norm_ffn=l0_norm_ffn, w_query=l0_peer_w_query, sub_keys=l0_peer_sub_keys,
              u=_pack_table(l0_peer_u), v=_pack_table(l0_peer_v))
    l1 = dict(norm_mix=l1_norm_mix, w_qkv=l1_attn_w_qkv, w_o=l1_attn_w_o,
              norm_ffn=l1_norm_ffn, w_query=l1_peer_w_query, sub_keys=l1_peer_sub_keys,
              u=_pack_table(l1_peer_u), v=_pack_table(l1_peer_v))
    return (_trunk(x_prompt, x_prompt.shape[1], l0, l1, norm_final),
            _trunk(x_sample, x_sample.shape[1], l0, l1, norm_final))
```

```python
import functools

import jax
import jax.numpy as jnp
from jax import lax
from jax.experimental import pallas as pl
from jax.experimental.pallas import tpu as pltpu

D_MODEL = 1024
RMS_EPS = 1e-6
TOKEN_TILE = 512
V7X_VMEM_LIMIT_BYTES = 56 * 1024 * 1024


def _params(n_grid_dims, vmem=None):
    return pltpu.CompilerParams(
        dimension_semantics=("arbitrary",) * n_grid_dims,
        vmem_limit_bytes=vmem)


def _rms(x, g):
    return x * lax.rsqrt(jnp.mean(x * x, axis=-1, keepdims=True) + RMS_EPS) * g


def _conv_in_kernel(x_ref, g_ref, w_ref, b_ref, z_ref):
    h = _rms(x_ref[...], g_ref[...]).astype(jnp.bfloat16)
    y = jnp.dot(h, w_ref[...], preferred_element_type=jnp.float32)
    d = D_MODEL
    b_ref[...] = y[:, :d]
    z_ref[...] = y[:, d:2 * d] * y[:, 2 * d:]


def _conv_in(x, g, w_in_bf16):
    n, d = x.shape
    tm = TOKEN_TILE
    return pl.pallas_call(
        _conv_in_kernel,
        grid=(n // tm,),
        in_specs=[
            pl.BlockSpec((tm, d), lambda i: (i, 0)),
            pl.BlockSpec((1, d), lambda i: (0, 0)),
            pl.BlockSpec((d, 3 * d), lambda i: (0, 0)),
        ],
        out_specs=[pl.BlockSpec((tm, d), lambda i: (i, 0)),
                   pl.BlockSpec((tm, d), lambda i: (i, 0))],
        out_shape=[jax.ShapeDtypeStruct((n, d), jnp.float32)] * 2,
        compiler_params=_params(1, V7X_VMEM_LIMIT_BYTES),
        name="conv_in",
    )(x, g.reshape(1, d), w_in_bf16)


def _conv_out_kernel(tiles_per_seq, x_ref, b_ref, z_ref, zp_ref, zn_ref, cw_ref, w_ref, o_ref):
    i = pl.program_id(0)
    pos = i % tiles_per_seq
    z = z_ref[...]
    tm = z.shape[0]
    row = lax.broadcasted_iota(jnp.int32, z.shape, 0)
    prev_row = jnp.where(pos == 0, 0.0, zp_ref[7:8, :])
    next_row = jnp.where(pos == tiles_per_seq - 1, 0.0, zn_ref[0:1, :])
    z_prev = jnp.where(row == 0, prev_row, pltpu.roll(z, 1, 0))
    z_next = jnp.where(row == tm - 1, next_row, pltpu.roll(z, tm - 1, 0))
    cw = cw_ref[...]
    conv = z_prev * cw[0:1, :] + z * cw[1:2, :] + z_next * cw[2:3, :]
    a = (b_ref[...] * conv).astype(jnp.bfloat16)
    o_ref[...] = x_ref[...] + jnp.dot(a, w_ref[...], preferred_element_type=jnp.float32)


def _conv_out(x, b, z, conv_w, w_out_bf16, seq_len):
    n, d = x.shape
    tm = TOKEN_TILE
    tps = seq_len // tm
    sub = tm // 8
    nb8 = n // 8
    tile = lambda i: (i, 0)
    return pl.pallas_call(
        functools.partial(_conv_out_kernel, tps),
        grid=(n // tm,),
        in_specs=[
            pl.BlockSpec((tm, d), tile),
            pl.BlockSpec((tm, d), tile),
            pl.BlockSpec((tm, d), tile),
            pl.BlockSpec((8, d), lambda i: (jnp.maximum(i * sub - 1, 0), 0)),
            pl.BlockSpec((8, d), lambda i: (jnp.minimum((i + 1) * sub, nb8 - 1), 0)),
            pl.BlockSpec((3, d), lambda i: (0, 0)),
            pl.BlockSpec((d, d), lambda i: (0, 0)),
        ],
        out_specs=pl.BlockSpec((tm, d), tile),
        out_shape=jax.ShapeDtypeStruct((n, d), jnp.float32),
        compiler_params=_params(1, V7X_VMEM_LIMIT_BYTES),
        name="conv_out",
    )(x, b, z, z, z, conv_w, w_out_bf16)


def conv_mixer_layer(x, g, w_in, conv_w, w_out, seq_len):
    b, z = _conv_in(x, g, w_in.astype(jnp.bfloat16))
    return _conv_out(x, b, z, conv_w, w_out.astype(jnp.bfloat16), seq_len)


DILATED_CONFIGS = ((128, 1), (512, 4), (2048, 16))
N_GROUPS = len(DILATED_CONFIGS)
HEADS_PER_GROUP = 16
HEAD_DIM = D_MODEL // HEADS_PER_GROUP
ATTN_SIDE = 64
ATTN_Q_BLOCK = 128
LSE_LANES = 128


def _qkv_kernel(x_ref, g_ref, w_ref, *refs):
    out_refs = refs[:N_GROUPS]
    h_scr, hp_scr = refs[N_GROUPS:]
    h = _rms(x_ref[...], g_ref[...])
    d = D_MODEL
    tm = x_ref.shape[0]
    for k in range(d // LANES):
        h_scr[k] = h[:, k * LANES:(k + 1) * LANES]
    for group, (_, dil) in enumerate(DILATED_CONFIGS):
        rows = tm // dil
        if dil == 1:
            hp = h.astype(jnp.bfloat16)
        else:
            for r in range(dil):
                for k in range(d // LANES):
                    hp_scr[r * rows:(r + 1) * rows, k * LANES:(k + 1) * LANES] = (
                        h_scr[k, pl.ds(r, rows, stride=dil), :].astype(jnp.bfloat16))
            hp = hp_scr[...]
        for c in range(3):
            col = (group * 3 + c) * d
            y = jnp.dot(hp, w_ref[:, col:col + d], preferred_element_type=jnp.float32).astype(jnp.bfloat16)
            for r in range(dil):
                out_refs[group][0, :, (r * 3 + c) * d:(r * 3 + c + 1) * d] = y[r * rows:(r + 1) * rows]


def _qkv_proj(x, g, w_bf16, seq_len):
    n, d = x.shape
    tm = TOKEN_TILE
    tps = seq_len // tm
    batch = n // seq_len
    return pl.pallas_call(
        _qkv_kernel,
        grid=(n // tm,),
        in_specs=[
            pl.BlockSpec((tm, d), lambda i: (i, 0)),
            pl.BlockSpec((1, d), lambda i: (0, 0)),
            pl.BlockSpec(w_bf16.shape, lambda i: (0, 0), pipeline_mode=pl.Buffered(1)),
        ],
        out_specs=[pl.BlockSpec((1, tm // dil, dil * 3 * d), lambda i: (i // tps, i % tps, 0))
                   for _, dil in DILATED_CONFIGS],
        out_shape=[jax.ShapeDtypeStruct((batch, seq_len // dil, dil * 3 * d), jnp.bfloat16)
                   for _, dil in DILATED_CONFIGS],
        scratch_shapes=[pltpu.VMEM((d // LANES, tm, LANES), jnp.float32),
                        pltpu.VMEM((tm, d), jnp.bfloat16)],
        compiler_params=_params(1, V7X_VMEM_LIMIT_BYTES),
        name="qkv_proj",
    )(x, g.reshape(1, d), w_bf16)


def _alibi_slope(head):
    n = N_GROUPS * HEADS_PER_GROUP
    return 2.0 ** (-8.0 * (head + 1) / n)


def _attn_kernel(group, dilation, seq_strided, q_ref, kp_ref, kc_ref, kn_ref,
                 vp_ref, vc_ref, vn_ref, o_ref, lse_ref, k_scr, v_scr):
    j = pl.program_id(2)
    qb = ATTN_Q_BLOCK
    nk = qb + 2 * ATTN_SIDE
    for scr, (p_ref, c_ref, n_ref) in ((k_scr, (kp_ref, kc_ref, kn_ref)), (v_scr, (vp_ref, vc_ref, vn_ref))):
        scr[0:ATTN_SIDE] = p_ref[0]
        scr[ATTN_SIDE:ATTN_SIDE + qb] = c_ref[0]
        scr[ATTN_SIDE + qb:nk] = n_ref[0]
    qi = lax.broadcasted_iota(jnp.int32, (qb, nk), 0)
    ci = lax.broadcasted_iota(jnp.int32, (qb, nk), 1)
    rel = ci - ATTN_SIDE - qi
    key_pos = j * qb - ATTN_SIDE + ci
    valid = (jnp.abs(rel) <= ATTN_SIDE) & (key_pos >= 0) & (key_pos < seq_strided)
    dist = jnp.where(valid, (dilation * jnp.abs(rel)).astype(jnp.float32), jnp.inf)
    lane = lax.broadcasted_iota(jnp.int32, (qb, LSE_LANES), 1)
    first = lane < HEAD_DIM
    pair_mask = [first.astype(jnp.bfloat16), (~first).astype(jnp.bfloat16)]
    lse_all = jnp.zeros((qb, LSE_LANES), jnp.float32)
    for pair in range(HEADS_PER_GROUP // 2):
        cols = slice(pair * LANES, (pair + 1) * LANES)
        q2, k2, v2 = q_ref[0, :, cols], k_scr[:, cols], v_scr[:, cols]
        outs = []
        for half in range(2):
            h = 2 * pair + half
            s = lax.dot_general(q2 * pair_mask[half], k2, (((1,), (1,)), ((), ())),
                                preferred_element_type=jnp.float32)
            slope = _alibi_slope(group * HEADS_PER_GROUP + h)
            s = s * (HEAD_DIM ** -0.5) - slope * dist
            m = jnp.max(s, axis=-1, keepdims=True)
            p = jnp.exp(s - m)
            den = jnp.sum(p, axis=-1, keepdims=True)
            o = jnp.dot(p.astype(jnp.bfloat16), v2, preferred_element_type=jnp.float32)
            outs.append(o / den)
            lse_all = jnp.where(lane == h, m + jnp.log(den), lse_all)
        o_ref[0, :, cols] = jnp.where(first, outs[0], outs[1])
    lse_ref[0] = lse_all


def _group_attention(view, group, dilation):
    d = D_MODEL
    batch, ls = view.shape[0], view.shape[1]
    width = 3
    qb = ATTN_Q_BLOCK
    sb = qb // ATTN_SIDE
    n_side = ls // ATTN_SIDE
    col0 = 0

    def cur(c):
        return pl.BlockSpec((1, qb, d), lambda b, r, j: (b, j, r * width + col0 + c))

    def prev(c):
        return pl.BlockSpec((1, ATTN_SIDE, d),
                            lambda b, r, j: (b, jnp.maximum(j * sb - 1, 0), r * width + col0 + c))

    def nxt(c):
        return pl.BlockSpec((1, ATTN_SIDE, d),
                            lambda b, r, j: (b, jnp.minimum((j + 1) * sb, n_side - 1), r * width + col0 + c))

    return pl.pallas_call(
        functools.partial(_attn_kernel, group, dilation, ls),
        grid=(batch, dilation, ls // qb),
        in_specs=[cur(0), prev(1), cur(1), nxt(1), prev(2), cur(2), nxt(2)],
        out_specs=[pl.BlockSpec((1, qb, d), lambda b, r, j: (b, j, r)),
                   pl.BlockSpec((1, qb, LSE_LANES), lambda b, r, j: (b, j, r))],
        out_shape=[jax.ShapeDtypeStruct((batch, ls, dilation * d), jnp.float32),
                   jax.ShapeDtypeStruct((batch, ls, dilation * LSE_LANES), jnp.float32)],
        scratch_shapes=[pltpu.VMEM((qb + 2 * ATTN_SIDE, d), jnp.bfloat16)] * 2,
        compiler_params=_params(3, V7X_VMEM_LIMIT_BYTES),
        name=f"attn_g{group}",
    )(view, view, view, view, view, view, view)


def _token_order(ref, width, dil, scr):
    if dil == 1:
        return ref[0]
    rows = ref.shape[1]
    chunks = width // LANES
    for r in range(dil):
        for k in range(chunks):
            col = r * width + k * LANES
            scr[k, pl.ds(r, rows, stride=dil), :] = ref[0, :, col:col + LANES]
    return jnp.concatenate([scr[k] for k in range(chunks)], axis=1)


def _merge_kernel(x_ref, o0_ref, o1_ref, o2_ref, l0_ref, l1_ref, l2_ref, e_ref, w_ref, out_ref,
                  o_scr, l_scr):
    dils = [dil for _, dil in DILATED_CONFIGS]
    l0, l1, l2 = [_token_order(ref, LSE_LANES, dil, l_scr)
                  for ref, dil in zip((l0_ref, l1_ref, l2_ref), dils)]
    m = jnp.maximum(jnp.maximum(l0, l1), l2)
    e0, e1, e2 = jnp.exp(l0 - m), jnp.exp(l1 - m), jnp.exp(l2 - m)
    tot = e0 + e1 + e2
    def expand(a):
        hi = a.astype(jnp.bfloat16)
        rest = a - hi.astype(jnp.float32)
        mid = rest.astype(jnp.bfloat16)
        lo = (rest - mid.astype(jnp.float32)).astype(jnp.bfloat16)
        return jnp.dot(jnp.concatenate([hi, mid, lo], axis=1), e_ref[...],
                       preferred_element_type=jnp.float32)
    merged = jnp.zeros(x_ref.shape, jnp.float32)
    for e, o_ref, dil in zip((e0, e1, e2), (o0_ref, o1_ref, o2_ref), dils):
        merged = merged + expand(e / tot) * _token_order(o_ref, D_MODEL, dil, o_scr)
    out_ref[...] = x_ref[...] + jnp.dot(merged.astype(jnp.bfloat16), w_ref[...],
                                        preferred_element_type=jnp.float32)


def _attn_merge(x, outs, lses, w_o_bf16, seq_len):
    n, d = x.shape
    tm = TOKEN_TILE
    tps = seq_len // tm
    head_of_col = jnp.arange(d, dtype=jnp.int32) // HEAD_DIM
    expand = (jnp.arange(LSE_LANES, dtype=jnp.int32)[:, None] == head_of_col[None, :]).astype(jnp.bfloat16)
    expand = jnp.concatenate([expand] * 3, axis=0)
    tile = lambda i: (i, 0)
    const = lambda i: (0, 0)
    strided = lambda i: (i // tps, i % tps, 0)
    dils = [dil for _, dil in DILATED_CONFIGS]
    return pl.pallas_call(
        _merge_kernel,
        grid=(n // tm,),
        in_specs=[pl.BlockSpec((tm, d), tile)]
        + [pl.BlockSpec((1, tm // dil, dil * d), strided) for dil in dils]
        + [pl.BlockSpec((1, tm // dil, dil * LSE_LANES), strided) for dil in dils]
        + [pl.BlockSpec((3 * LSE_LANES, d), const), pl.BlockSpec((d, d), const)],
        out_specs=pl.BlockSpec((tm, d), tile),
        out_shape=jax.ShapeDtypeStruct((n, d), jnp.float32),
        scratch_shapes=[pltpu.VMEM((d // LANES, tm, LANES), jnp.float32),
                        pltpu.VMEM((1, tm, LSE_LANES), jnp.float32)],
        compiler_params=_params(1, V7X_VMEM_LIMIT_BYTES),
        name="attn_merge",
    )(x, *outs, *lses, expand, w_o_bf16)


def attn_mixer_layer(x, g, w_qkv, w_o, seq_len):
    views = _qkv_proj(x, g, w_qkv.astype(jnp.bfloat16), seq_len)
    outs, lses = [], []
    for group, (window, dilation) in enumerate(DILATED_CONFIGS):
        assert window // (2 * dilation) == ATTN_SIDE
        o, lse = _group_attention(views[group], group, dilation)
        outs.append(o)
        lses.append(lse)
    return _attn_merge(x, outs, lses, w_o.astype(jnp.bfloat16), seq_len)


PEER_HEADS = 8
PEER_HALF = 128
N_KEYS = 128
PEER_TOPK = 16
PEER_TILE = 128
PEER_SELECT_TILE = 512
SELECT_HEADS_PER_STEP = 2
OUT_TOKENS_PER_STEP = 4
ACT_TOKENS_PER_STEP = 4
ACT_REDUCE_UNROLL = 16
SLOTS = PEER_HEADS * PEER_TOPK
LANES = 128
HALF_D = D_MODEL // 2
ROWS_PER_EXPERT = HALF_D // LANES
TABLE_PAD_ROWS = 8 * ROWS_PER_EXPERT
SUBLANES = 8


def _take_top(s, ids, n):
    vals, picks = [], []
    for _ in range(n):
        m = jnp.max(s, axis=0, keepdims=True)
        sel = jnp.min(jnp.where(s == m, ids, jnp.int32(2 ** 30)), axis=0, keepdims=True)
        s = jnp.where(ids == sel, -jnp.inf, s)
        vals.append(m)
        picks.append(sel)
    return vals, picks


def _peer_select_kernel(x_ref, g_ref, wq_ref, keys_ref, h_ref, idx_ref, gate_ref, q_scr):
    step = pl.program_id(1)
    t = x_ref.shape[0]

    @pl.when(step == 0)
    def _():
        h = _rms(x_ref[...], g_ref[...])
        h_ref[...] = h
        q = jnp.dot(h.astype(jnp.bfloat16), wq_ref[...], preferred_element_type=jnp.float32)
        for k in range(PEER_HEADS):
            q_scr[k] = q[:, k * 2 * PEER_HALF:(k + 1) * 2 * PEER_HALF].astype(jnp.bfloat16)

    for j in range(SELECT_HEADS_PER_STEP):
        q = q_scr[step * SELECT_HEADS_PER_STEP + j]
        gate, idx = _select_head(q, keys_ref[j, 0], keys_ref[j, 1], t)
        gate_ref[0, j * PEER_TOPK:(j + 1) * PEER_TOPK, :] = gate
        idx_ref[j] = idx


def _select_head(q, keys1, keys2, t):
    key_id = lax.broadcasted_iota(jnp.int32, (N_KEYS, t), 0)
    halves = []
    for p, keys in enumerate((keys1, keys2)):
        s = lax.dot_general(keys, q[:, p * PEER_HALF:(p + 1) * PEER_HALF],
                            (((1,), (1,)), ((), ())), preferred_element_type=jnp.float32)
        vals, ids = _take_top(s, key_id, PEER_TOPK)
        halves.append((jnp.concatenate(vals, axis=0), jnp.concatenate(ids, axis=0)))
    (s1, i1), (s2, i2) = halves

    n_experts = N_KEYS * N_KEYS
    cv, cid = [], []
    for b in range(SUBLANES):
        rows = PEER_TOPK if b == 0 else SUBLANES
        a_id = lax.broadcasted_iota(jnp.int32, (rows, t), 0)
        v = s1[0:rows] + s2[b:b + 1]
        na = PEER_TOPK // (b + 1)
        if na < rows:
            v = jnp.where(a_id < na, v, -jnp.inf)
        cv.append(v)
        cid.append((a_id * PEER_TOPK + b) * n_experts + i1[0:rows] * N_KEYS + i2[b:b + 1])
    b_id = lax.broadcasted_iota(jnp.int32, (SUBLANES, t), 0) + SUBLANES
    cv.append(s1[0:1] + s2[SUBLANES:])
    cid.append(b_id * n_experts + i1[0:1] * N_KEYS + i2[SUBLANES:])
    top, picks = _take_top(jnp.concatenate(cv, axis=0), jnp.concatenate(cid, axis=0), PEER_TOPK)
    top = jnp.concatenate(top, axis=0)
    e = jnp.exp(top - top[0:1])
    experts = jnp.concatenate(picks, axis=0) & (n_experts - 1)
    return e / jnp.sum(e, axis=0, keepdims=True), experts * ROWS_PER_EXPERT + TABLE_PAD_ROWS


def _peer_select(x, g, wq_bf16, keys_bf16):
    n, d = x.shape
    t = PEER_SELECT_TILE
    nt = n // t
    hps = SELECT_HEADS_PER_STEP
    return pl.pallas_call(
        _peer_select_kernel,
        grid=(nt, PEER_HEADS // hps),
        in_specs=[
            pl.BlockSpec((t, d), lambda i, h: (i, 0)),
            pl.BlockSpec((1, d), lambda i, h: (0, 0)),
            pl.BlockSpec(wq_bf16.shape, lambda i, h: (0, 0)),
            pl.BlockSpec((hps, 2, N_KEYS, PEER_HALF), lambda i, h: (h, 0, 0, 0)),
        ],
        out_specs=[pl.BlockSpec((t, d), lambda i, h: (i, 0)),
                   pl.BlockSpec((hps, PEER_TOPK, t), lambda i, h: (h, i, 0)),
                   pl.BlockSpec((1, hps * PEER_TOPK, t), lambda i, h: (i, h, 0))],
        out_shape=[jax.ShapeDtypeStruct((n, d), jnp.float32),
                   jax.ShapeDtypeStruct((PEER_HEADS, nt * PEER_TOPK, t), jnp.int32),
                   jax.ShapeDtypeStruct((nt, SLOTS, t), jnp.float32)],
        scratch_shapes=[pltpu.VMEM((PEER_HEADS, t, 2 * PEER_HALF), jnp.bfloat16)],
        compiler_params=_params(2, V7X_VMEM_LIMIT_BYTES),
        name="peer_select",
    )(x, g.reshape(1, d), wq_bf16, keys_bf16)


def _pack_table(tab):
    zero_experts = jnp.zeros((TABLE_PAD_ROWS // ROWS_PER_EXPERT, tab.shape[1]), tab.dtype)
    tab = jnp.concatenate([zero_experts, tab, zero_experts], axis=0)
    bits = lax.bitcast_convert_type(tab.astype(jnp.bfloat16), jnp.uint16).astype(jnp.uint32)
    packed = bits[:, :HALF_D] | (bits[:, HALF_D:] << 16)
    return packed.reshape(tab.shape[0] * ROWS_PER_EXPERT, LANES)


def _load_expert_pair(tab_ref, head_idx_ref, k, col, low_sublanes):
    ra = pl.multiple_of(head_idx_ref[k * PEER_SELECT_TILE + col], ROWS_PER_EXPERT)
    rb = pl.multiple_of(head_idx_ref[(k + 1) * PEER_SELECT_TILE + col] - ROWS_PER_EXPERT, ROWS_PER_EXPERT)
    pair = jnp.where(low_sublanes, tab_ref[pl.ds(ra, SUBLANES), :], tab_ref[pl.ds(rb, SUBLANES), :])
    lo = pltpu.bitcast(pair << 16, jnp.float32)
    hi = pltpu.bitcast(pair & jnp.uint32(0xFFFF0000), jnp.float32)
    return lo, hi


def _expert_pairs(tab_ref, idx_refs, t, low_sublanes):
    per_select = PEER_SELECT_TILE // PEER_TILE
    col = (pl.program_id(0) % per_select) * PEER_TILE + t
    for kk in range(PEER_TOPK // 2):
        for hd in range(PEER_HEADS):
            lo, hi = _load_expert_pair(tab_ref, idx_refs[hd], 2 * kk, col, low_sublanes)
            yield hd * (PEER_TOPK // 2) + kk, lo, hi


def _to_token_rows(src_ref, rows_scr):
    t = src_ref.shape[0]
    for c in range(SUBLANES):
        rows_scr[pl.ds(c, t, stride=SUBLANES), :] = src_ref[:, c * LANES:(c + 1) * LANES]


def _peer_act_kernel(*refs):
    idx_refs = refs[:PEER_HEADS]
    h_ref, tab_ref, gate_ref, w_ref, hrow_scr, act_scr = refs[PEER_HEADS:PEER_HEADS + 6]
    prod_scrs = refs[PEER_HEADS + 6:]
    group = len(prod_scrs)
    t_tile = PEER_TILE
    lane = lax.broadcasted_iota(jnp.int32, (SLOTS, t_tile), 1)
    low = lax.broadcasted_iota(jnp.int32, (SUBLANES, LANES), 0) < ROWS_PER_EXPERT
    _to_token_rows(h_ref, hrow_scr)
    for prod_scr in prod_scrs:
        prod_scr[...] = jnp.zeros(prod_scr.shape, jnp.float32)
    act_scr[...] = jnp.zeros(act_scr.shape, jnp.float32)

    def products(t, prod_scr):
        hv = hrow_scr[pl.ds(pl.multiple_of(t * SUBLANES, SUBLANES), SUBLANES), :]
        swapped = pltpu.roll(hv, ROWS_PER_EXPERT, 0)
        h_lo = jnp.where(low, hv, swapped)
        h_hi = jnp.where(low, swapped, hv)
        for p, lo, hi in _expert_pairs(tab_ref, idx_refs, t, low):
            prod_scr[p * SUBLANES:(p + 1) * SUBLANES, :] = lo * h_lo + hi * h_hi

    def dots(prod_scr):
        part = prod_scr[pl.ds(0, SLOTS, stride=ROWS_PER_EXPERT), :]
        for k in range(1, ROWS_PER_EXPERT):
            part = part + prod_scr[pl.ds(k, SLOTS, stride=ROWS_PER_EXPERT), :]
        return jnp.sum(part, axis=1, keepdims=True)

    def with_dots(act, t0):
        for k, prod_scr in enumerate(prod_scrs):
            act = jnp.where(lane == t0 + k, dots(prod_scr), act)
        return act

    def token_group(i, carry):
        t0 = group * i
        act_scr[...] = with_dots(act_scr[...], t0 - group)
        for k, prod_scr in enumerate(prod_scrs):
            products(t0 + k, prod_scr)
        return carry

    lax.fori_loop(0, t_tile // group, token_group, 0)
    act = with_dots(act_scr[...], t_tile - group)
    gelu = 0.5 * act * (1.0 + lax.erf(act * (0.5 ** 0.5)))
    w_ref[0] = gate_ref[0] * gelu


def _head_idx_specs():
    per_select = PEER_SELECT_TILE // PEER_TILE
    return [pl.BlockSpec((PEER_TOPK * PEER_SELECT_TILE,), lambda i: (i // per_select,),
                         memory_space=pltpu.SMEM)] * PEER_HEADS


def _peer_act(head_idx, h, u_packed, gate):
    t = PEER_TILE
    n, d = h.shape
    nt = n // t
    per_gate = PEER_SELECT_TILE // PEER_TILE
    prod = pltpu.VMEM((SLOTS * ROWS_PER_EXPERT, LANES), jnp.float32)
    return pl.pallas_call(
        _peer_act_kernel,
        grid=(nt,),
        in_specs=_head_idx_specs() + [
            pl.BlockSpec((t, d), lambda i: (i, 0)),
            pl.BlockSpec(u_packed.shape, lambda i: (0, 0), pipeline_mode=pl.Buffered(1)),
            pl.BlockSpec((1, SLOTS, t), lambda i: (i // per_gate, 0, i % per_gate)),
        ],
        out_specs=pl.BlockSpec((1, SLOTS, t), lambda i: (i, 0, 0)),
        out_shape=jax.ShapeDtypeStruct((nt, SLOTS, t), jnp.float32),
        scratch_shapes=[pltpu.VMEM((t * SUBLANES, LANES), jnp.float32),
                        pltpu.VMEM((SLOTS, t), jnp.float32)] + [prod] * ACT_TOKENS_PER_STEP,
        compiler_params=_params(1, V7X_VMEM_LIMIT_BYTES),
        name="peer_act",
    )(*head_idx, h, u_packed, gate)


def _peer_out_kernel(final_norm, *refs):
    idx_refs = refs[:PEER_HEADS]
    w_ref, w_next_ref, x_ref, tab_ref, g_ref, o_ref, xrow_scr, wb_even, wb_odd = refs[PEER_HEADS:]
    step = pl.program_id(0)
    t_tile = PEER_TILE
    lane = lax.broadcasted_iota(jnp.int32, (SLOTS, t_tile), 1)
    low = lax.broadcasted_iota(jnp.int32, (SUBLANES, LANES), 0) < ROWS_PER_EXPERT
    _to_token_rows(x_ref, xrow_scr)

    def weight_rows(t, tile_w_ref):
        w_col = jnp.sum(jnp.where(lane == t, tile_w_ref[0], 0.0), axis=1, keepdims=True)
        return jnp.broadcast_to(w_col, (SLOTS, LANES))

    def accumulate(t, wb_ref):
        acc_lo = jnp.zeros((SUBLANES, LANES), jnp.float32)
        acc_hi = jnp.zeros((SUBLANES, LANES), jnp.float32)
        for p, lo, hi in _expert_pairs(tab_ref, idx_refs, t, low):
            w2 = jnp.where(low, jnp.broadcast_to(wb_ref[2 * p:2 * p + 1, :], (SUBLANES, LANES)),
                           jnp.broadcast_to(wb_ref[2 * p + 1:2 * p + 2, :], (SUBLANES, LANES)))
            acc_lo = acc_lo + w2 * lo
            acc_hi = acc_hi + w2 * hi
        y = jnp.where(low, acc_lo + pltpu.roll(acc_lo, ROWS_PER_EXPERT, 0),
                      acc_hi + pltpu.roll(acc_hi, ROWS_PER_EXPERT, 0))
        rows = pl.ds(pl.multiple_of(t * SUBLANES, SUBLANES), SUBLANES)
        xrow_scr[rows, :] = xrow_scr[rows, :] + y

    @pl.when(step == 0)
    def _():
        def spread_weights(i, carry):
            for k in range(ACT_REDUCE_UNROLL):
                t = i * ACT_REDUCE_UNROLL + k
                wb_even[t] = weight_rows(t, w_ref)
            return carry

        lax.fori_loop(0, t_tile // ACT_REDUCE_UNROLL, spread_weights, 0)

    def run(wb_cur, wb_next):
        def token_group(i, carry):
            for k in range(OUT_TOKENS_PER_STEP):
                t = i * OUT_TOKENS_PER_STEP + k
                accumulate(t, wb_cur.at[t])
                wb_next[t] = weight_rows(t, w_next_ref)
            return carry

        lax.fori_loop(0, t_tile // OUT_TOKENS_PER_STEP, token_group, 0)

    @pl.when(step % 2 == 0)
    def _():
        run(wb_even, wb_odd)

    @pl.when(step % 2 == 1)
    def _():
        run(wb_odd, wb_even)

    for c in range(SUBLANES):
        o_ref[:, c * LANES:(c + 1) * LANES] = xrow_scr[pl.ds(c, t_tile, stride=SUBLANES), :]
    if final_norm:
        o_ref[...] = _rms(o_ref[...], g_ref[...])


def _peer_out(head_idx, w, x, v_packed, g_final):
    t = PEER_TILE
    n, d = x.shape
    nt = n // t
    g = jnp.ones((d,), jnp.float32) if g_final is None else g_final
    wb = pltpu.VMEM((t, SLOTS, LANES), jnp.float32)
    return pl.pallas_call(
        functools.partial(_peer_out_kernel, g_final is not None),
        grid=(nt,),
        in_specs=_head_idx_specs() + [
            pl.BlockSpec((1, SLOTS, t), lambda i: (i, 0, 0)),
            pl.BlockSpec((1, SLOTS, t), lambda i: (jnp.minimum(i + 1, nt - 1), 0, 0)),
            pl.BlockSpec((t, d), lambda i: (i, 0)),
            pl.BlockSpec(v_packed.shape, lambda i: (0, 0), pipeline_mode=pl.Buffered(1)),
            pl.BlockSpec((1, d), lambda i: (0, 0)),
        ],
        out_specs=pl.BlockSpec((t, d), lambda i: (i, 0)),
        out_shape=jax.ShapeDtypeStruct((n, d), jnp.float32),
        scratch_shapes=[pltpu.VMEM((t * SUBLANES, LANES), jnp.float32), wb, wb],
        compiler_params=_params(1, V7X_VMEM_LIMIT_BYTES),
        name="peer_out",
    )(*head_idx, w, w, x, v_packed, g.reshape(1, d))


def peer_layer(x, g, w_query, sub_keys, u_packed, v_packed, g_final=None):
    h, idx, gate = _peer_select(x, g, w_query.astype(jnp.bfloat16), sub_keys.astype(jnp.bfloat16))
    flat = idx.reshape(PEER_HEADS, -1)
    head_idx = [flat[hd] for hd in range(PEER_HEADS)]
    w = _peer_act(head_idx, h, u_packed, gate)
    return _peer_out(head_idx, w, x, v_packed, g_final)


def _trunk(x, seq_len, l0, l1, g_final):
    b = x.shape[0]
    x = x.reshape(b * seq_len, D_MODEL)
    x = conv_mixer_layer(x, l0["norm_mix"], l0["w_in"], l0["conv_w"], l0["w_out"], seq_len)
    x = peer_layer(x, l0["norm_ffn"], l0["w_query"], l0["sub_keys"], l0["u"], l0["v"])
    x = attn_mixer_layer(x, l1["norm_mix"], l1["w_qkv"], l1["w_o"], seq_len)
    x = peer_layer(x, l1["norm_ffn"], l1["w_query"], l1["sub_keys"], l1["u"], l1["v"], g_final)
    return x.reshape(b, seq_len, D_MODEL)


def kernel(x_prompt, x_sample, l0_norm_mix, l0_conv_w_in, l0_conv_w, l0_conv_w_out, l0_norm_ffn, l0_peer_w_query, l0_peer_sub_keys, l0_peer_u, l0_peer_v, l1_norm_mix, l1_attn_w_qkv, l1_attn_w_o, l1_norm_ffn, l1_peer_w_query, l1_peer_sub_keys, l1_peer_u, l1_peer_v, norm_final):
    l0 = dict(norm_mix=l0_norm_mix, w_in=l0_conv_w_in, conv_w=l0_conv_w, w_out=l0_conv_w_out,
              norm_ffn=l0_norm_ffn, w_query=l0_peer_w_query, sub_keys=l0_peer_sub_keys,
              u=_pack_table(l0_peer_u), v=_pack_table(l0_peer_v))
    l1 = dict(norm_mix=l1_norm_mix, w_qkv=l1_attn_w_qkv, w_o=l1_attn_w_o,
              norm_ffn=l1_norm_ffn, w_query=l1_peer_w_query, sub_keys=l1_peer_sub_keys,
              u=_pack_table(l1_peer_u), v=_pack_table(l1_peer_v))
    return (_trunk(x_prompt, x_prompt.shape[1], l0, l1, norm_final),
            _trunk(x_sample, x_sample.shape[1], l0, l1, norm_final))
```

```python
import functools

import jax
import jax.numpy as jnp
from jax import lax
from jax.experimental import pallas as pl
from jax.experimental.pallas import tpu as pltpu

D_MODEL = 1024
RMS_EPS = 1e-6
TOKEN_TILE = 512
V7X_VMEM_LIMIT_BYTES = 56 * 1024 * 1024


def _params(n_grid_dims, vmem=None):
    return pltpu.CompilerParams(
        dimension_semantics=("arbitrary",) * n_grid_dims,
        vmem_limit_bytes=vmem)


def _rms(x, g):
    return x * lax.rsqrt(jnp.mean(x * x, axis=-1, keepdims=True) + RMS_EPS) * g


def _conv_in_kernel(x_ref, g_ref, w_ref, b_ref, z_ref):
    h = _rms(x_ref[...], g_ref[...]).astype(jnp.bfloat16)
    y = jnp.dot(h, w_ref[...], preferred_element_type=jnp.float32)
    d = D_MODEL
    b_ref[...] = y[:, :d]
    z_ref[...] = y[:, d:2 * d] * y[:, 2 * d:]


def _conv_in(x, g, w_in_bf16):
    n, d = x.shape
    tm = TOKEN_TILE
    return pl.pallas_call(
        _conv_in_kernel,
        grid=(n // tm,),
        in_specs=[
            pl.BlockSpec((tm, d), lambda i: (i, 0)),
            pl.BlockSpec((1, d), lambda i: (0, 0)),
            pl.BlockSpec((d, 3 * d), lambda i: (0, 0)),
        ],
        out_specs=[pl.BlockSpec((tm, d), lambda i: (i, 0)),
                   pl.BlockSpec((tm, d), lambda i: (i, 0))],
        out_shape=[jax.ShapeDtypeStruct((n, d), jnp.float32)] * 2,
        compiler_params=_params(1, V7X_VMEM_LIMIT_BYTES),
        name="conv_in",
    )(x, g.reshape(1, d), w_in_bf16)


def _conv_out_kernel(tiles_per_seq, x_ref, b_ref, z_ref, zp_ref, zn_ref, cw_ref, w_ref, o_ref):
    i = pl.program_id(0)
    pos = i % tiles_per_seq
    z = z_ref[...]
    tm = z.shape[0]
    row = lax.broadcasted_iota(jnp.int32, z.shape, 0)
    prev_row = jnp.where(pos == 0, 0.0, zp_ref[7:8, :])
    next_row = jnp.where(pos == tiles_per_seq - 1, 0.0, zn_ref[0:1, :])
    z_prev = jnp.where(row == 0, prev_row, pltpu.roll(z, 1, 0))
    z_next = jnp.where(row == tm - 1, next_row, pltpu.roll(z, tm - 1, 0))
    cw = cw_ref[...]
    conv = z_prev * cw[0:1, :] + z * cw[1:2, :] + z_next * cw[2:3, :]
    a = (b_ref[...] * conv).astype(jnp.bfloat16)
    o_ref[...] = x_ref[...] + jnp.dot(a, w_ref[...], preferred_element_type=jnp.float32)


def _conv_out(x, b, z, conv_w, w_out_bf16, seq_len):
    n, d = x.shape
    tm = TOKEN_TILE
    tps = seq_len // tm
    sub = tm // 8
    nb8 = n // 8
    tile = lambda i: (i, 0)
    return pl.pallas_call(
        functools.partial(_conv_out_kernel, tps),
        grid=(n // tm,),
        in_specs=[
            pl.BlockSpec((tm, d), tile),
            pl.BlockSpec((tm, d), tile),
            pl.BlockSpec((tm, d), tile),
            pl.BlockSpec((8, d), lambda i: (jnp.maximum(i * sub - 1, 0), 0)),
            pl.BlockSpec((8, d), lambda i: (jnp.minimum((i + 1) * sub, nb8 - 1), 0)),
            pl.BlockSpec((3, d), lambda i: (0, 0)),
            pl.BlockSpec((d, d), lambda i: (0, 0)),
        ],
        out_specs=pl.BlockSpec((tm, d), tile),
        out_shape=jax.ShapeDtypeStruct((n, d), jnp.float32),
        compiler_params=_params(1, V7X_VMEM_LIMIT_BYTES),
        name="conv_out",
    )(x, b, z, z, z, conv_w, w_out_bf16)


def conv_mixer_layer(x, g, w_in, conv_w, w_out, seq_len):
    b, z = _conv_in(x, g, w_in.astype(jnp.bfloat16))
    return _conv_out(x, b, z, conv_w, w_out.astype(jnp.bfloat16), seq_len)


DILATED_CONFIGS = ((128, 1), (512, 4), (2048, 16))
N_GROUPS = len(DILATED_CONFIGS)
HEADS_PER_GROUP = 16
HEAD_DIM = D_MODEL // HEADS_PER_GROUP
ATTN_SIDE = 64
ATTN_Q_BLOCK = 128
LSE_LANES = 128


def _qkv_kernel(x_ref, g_ref, w_ref, *refs):
    out_refs = refs[:N_GROUPS]
    h_scr, hp_scr = refs[N_GROUPS:]
    h = _rms(x_ref[...], g_ref[...])
    d = D_MODEL
    tm = x_ref.shape[0]
    for k in range(d // LANES):
        h_scr[k] = h[:, k * LANES:(k + 1) * LANES]
    for group, (_, dil) in enumerate(DILATED_CONFIGS):
        rows = tm // dil
        if dil == 1:
            hp = h.astype(jnp.bfloat16)
        else:
            for r in range(dil):
                for k in range(d // LANES):
                    hp_scr[r * rows:(r + 1) * rows, k * LANES:(k + 1) * LANES] = (
                        h_scr[k, pl.ds(r, rows, stride=dil), :].astype(jnp.bfloat16))
            hp = hp_scr[...]
        for c in range(3):
            col = (group * 3 + c) * d
            y = jnp.dot(hp, w_ref[:, col:col + d], preferred_element_type=jnp.float32).astype(jnp.bfloat16)
            for r in range(dil):
                out_refs[group][0, :, (r * 3 + c) * d:(r * 3 + c + 1) * d] = y[r * rows:(r + 1) * rows]


def _qkv_proj(x, g, w_bf16, seq_len):
    n, d = x.shape
    tm = TOKEN_TILE
    tps = seq_len // tm
    batch = n // seq_len
    return pl.pallas_call(
        _qkv_kernel,
        grid=(n // tm,),
        in_specs=[
            pl.BlockSpec((tm, d), lambda i: (i, 0)),
            pl.BlockSpec((1, d), lambda i: (0, 0)),
            pl.BlockSpec(w_bf16.shape, lambda i: (0, 0), pipeline_mode=pl.Buffered(1)),
        ],
        out_specs=[pl.BlockSpec((1, tm // dil, dil * 3 * d), lambda i: (i // tps, i % tps, 0))
                   for _, dil in DILATED_CONFIGS],
        out_shape=[jax.ShapeDtypeStruct((batch, seq_len // dil, dil * 3 * d), jnp.bfloat16)
                   for _, dil in DILATED_CONFIGS],
        scratch_shapes=[pltpu.VMEM((d // LANES, tm, LANES), jnp.float32),
                        pltpu.VMEM((tm, d), jnp.bfloat16)],
        compiler_params=_params(1, V7X_VMEM_LIMIT_BYTES),
        name="qkv_proj",
    )(x, g.reshape(1, d), w_bf16)


def _alibi_slope(head):
    n = N_GROUPS * HEADS_PER_GROUP
    return 2.0 ** (-8.0 * (head + 1) / n)


def _attn_kernel(group, dilation, seq_strided, q_ref, kp_ref, kc_ref, kn_ref,
                 vp_ref, vc_ref, vn_ref, o_ref, lse_ref, k_scr, v_scr):
    j = pl.program_id(2)
    qb = ATTN_Q_BLOCK
    nk = qb + 2 * ATTN_SIDE
    for scr, (p_ref, c_ref, n_ref) in ((k_scr, (kp_ref, kc_ref, kn_ref)), (v_scr, (vp_ref, vc_ref, vn_ref))):
        scr[0:ATTN_SIDE] = p_ref[0]
        scr[ATTN_SIDE:ATTN_SIDE + qb] = c_ref[0]
        scr[ATTN_SIDE + qb:nk] = n_ref[0]
    qi = lax.broadcasted_iota(jnp.int32, (qb, nk), 0)
    ci = lax.broadcasted_iota(jnp.int32, (qb, nk), 1)
    rel = ci - ATTN_SIDE - qi
    key_pos = j * qb - ATTN_SIDE + ci
    valid = (jnp.abs(rel) <= ATTN_SIDE) & (key_pos >= 0) & (key_pos < seq_strided)
    dist = jnp.where(valid, (dilation * jnp.abs(rel)).astype(jnp.float32), jnp.inf)
    lane = lax.broadcasted_iota(jnp.int32, (qb, LSE_LANES), 1)
    first = lane < HEAD_DIM
    pair_mask = [first.astype(jnp.bfloat16), (~first).astype(jnp.bfloat16)]
    lse_all = jnp.zeros((qb, LSE_LANES), jnp.float32)
    for pair in range(HEADS_PER_GROUP // 2):
        cols = slice(pair * LANES, (pair + 1) * LANES)
        q2, k2, v2 = q_ref[0, :, cols], k_scr[:, cols], v_scr[:, cols]
        outs = []
        for half in range(2):
            h = 2 * pair + half
            s = lax.dot_general(q2 * pair_mask[half], k2, (((1,), (1,)), ((), ())),
                                preferred_element_type=jnp.float32)
            slope = _alibi_slope(group * HEADS_PER_GROUP + h)
            s = s * (HEAD_DIM ** -0.5) - slope * dist
            m = jnp.max(s, axis=-1, keepdims=True)
            p = jnp.exp(s - m)
            den = jnp.sum(p, axis=-1, keepdims=True)
            o = jnp.dot(p.astype(jnp.bfloat16), v2, preferred_element_type=jnp.float32)
            outs.append(o / den)
            lse_all = jnp.where(lane == h, m + jnp.log(den), lse_all)
        o_ref[0, :, cols] = jnp.where(first, outs[0], outs[1])
    lse_ref[0] = lse_all


def _group_attention(view, group, dilation):
    d = D_MODEL
    batch, ls = view.shape[0], view.shape[1]
    width = 3
    qb = ATTN_Q_BLOCK
    sb = qb // ATTN_SIDE
    n_side = ls // ATTN_SIDE
    col0 = 0

    def cur(c):
        return pl.BlockSpec((1, qb, d), lambda b, r, j: (b, j, r * width + col0 + c))

    def prev(c):
        return pl.BlockSpec((1, ATTN_SIDE, d),
                            lambda b, r, j: (b, jnp.maximum(j * sb - 1, 0), r * width + col0 + c))

    def nxt(c):
        return pl.BlockSpec((1, ATTN_SIDE, d),
                            lambda b, r, j: (b, jnp.minimum((j + 1) * sb, n_side - 1), r * width + col0 + c))

    return pl.pallas_call(
        functools.partial(_attn_kernel, group, dilation, ls),
        grid=(batch, dilation, ls // qb),
        in_specs=[cur(0), prev(1), cur(1), nxt(1), prev(2), cur(2), nxt(2)],
        out_specs=[pl.BlockSpec((1, qb, d), lambda b, r, j: (b, j, r)),
                   pl.BlockSpec((1, qb, LSE_LANES), lambda b, r, j: (b, j, r))],
        out_shape=[jax.ShapeDtypeStruct((batch, ls, dilation * d), jnp.float32),
                   jax.ShapeDtypeStruct((batch, ls, dilation * LSE_LANES), jnp.float32)],
        scratch_shapes=[pltpu.VMEM((qb + 2 * ATTN_SIDE, d), jnp.bfloat16)] * 2,
        compiler_params=_params(3, V7X_VMEM_LIMIT_BYTES),
        name=f"attn_g{group}",
    )(view, view, view, view, view, view, view)


def _token_order(ref, width, dil, scr):
    if dil == 1:
        return ref[0]
    rows = ref.shape[1]
    chunks = width // LANES
    for r in range(dil):
        for k in range(chunks):
            col = r * width + k * LANES
            scr[k, pl.ds(r, rows, stride=dil), :] = ref[0, :, col:col + LANES]
    return jnp.concatenate([scr[k] for k in range(chunks)], axis=1)


def _merge_kernel(x_ref, o0_ref, o1_ref, o2_ref, l0_ref, l1_ref, l2_ref, e_ref, w_ref, out_ref,
                  o_scr, l_scr):
    dils = [dil for _, dil in DILATED_CONFIGS]
    l0, l1, l2 = [_token_order(ref, LSE_LANES, dil, l_scr)
                  for ref, dil in zip((l0_ref, l1_ref, l2_ref), dils)]
    m = jnp.maximum(jnp.maximum(l0, l1), l2)
    e0, e1, e2 = jnp.exp(l0 - m), jnp.exp(l1 - m), jnp.exp(l2 - m)
    tot = e0 + e1 + e2
    def expand(a):
        hi = a.astype(jnp.bfloat16)
        rest = a - hi.astype(jnp.float32)
        mid = rest.astype(jnp.bfloat16)
        lo = (rest - mid.astype(jnp.float32)).astype(jnp.bfloat16)
        return jnp.dot(jnp.concatenate([hi, mid, lo], axis=1), e_ref[...],
                       preferred_element_type=jnp.float32)
    merged = jnp.zeros(x_ref.shape, jnp.float32)
    for e, o_ref, dil in zip((e0, e1, e2), (o0_ref, o1_ref, o2_ref), dils):
        merged = merged + expand(e / tot) * _token_order(o_ref, D_MODEL, dil, o_scr)
    out_ref[...] = x_ref[...] + jnp.dot(merged.astype(jnp.bfloat16), w_ref[...],
                                        preferred_element_type=jnp.float32)


def _attn_merge(x, outs, lses, w_o_bf16, seq_len):
    n, d = x.shape
    tm = TOKEN_TILE
    tps = seq_len // tm
    head_of_col = jnp.arange(d, dtype=jnp.int32) // HEAD_DIM
    expand = (jnp.arange(LSE_LANES, dtype=jnp.int32)[:, None] == head_of_col[None, :]).astype(jnp.bfloat16)
    expand = jnp.concatenate([expand] * 3, axis=0)
    tile = lambda i: (i, 0)
    const = lambda i: (0, 0)
    strided = lambda i: (i // tps, i % tps, 0)
    dils = [dil for _, dil in DILATED_CONFIGS]
    return pl.pallas_call(
        _merge_kernel,
        grid=(n // tm,),
        in_specs=[pl.BlockSpec((tm, d), tile)]
        + [pl.BlockSpec((1, tm // dil, dil * d), strided) for dil in dils]
        + [pl.BlockSpec((1, tm // dil, dil * LSE_LANES), strided) for dil in dils]
        + [pl.BlockSpec((3 * LSE_LANES, d), const), pl.BlockSpec((d, d), const)],
        out_specs=pl.BlockSpec((tm, d), tile),
        out_shape=jax.ShapeDtypeStruct((n, d), jnp.float32),
        scratch_shapes=[pltpu.VMEM((d // LANES, tm, LANES), jnp.float32),
                        pltpu.VMEM((1, tm, LSE_LANES), jnp.float32)],
        compiler_params=_params(1, V7X_VMEM_LIMIT_BYTES),
        name="attn_merge",
    )(x, *outs, *lses, expand, w_o_bf16)


def attn_mixer_layer(x, g, w_qkv, w_o, seq_len):
    views = _qkv_proj(x, g, w_qkv.astype(jnp.bfloat16), seq_len)
    outs, lses = [], []
    for group, (window, dilation) in enumerate(DILATED_CONFIGS):
        assert window // (2 * dilation) == ATTN_SIDE
        o, lse = _group_attention(views[group], group, dilation)
        outs.append(o)
        lses.append(lse)
    return _attn_merge(x, outs, lses, w_o.astype(jnp.bfloat16), seq_len)


PEER_HEADS = 8
PEER_HALF = 128
N_KEYS = 128
PEER_TOPK = 16
PEER_TILE = 128
PEER_SELECT_TILE = 512
SELECT_HEADS_PER_STEP = 2
OUT_TOKENS_PER_STEP = 8
ACT_TOKENS_PER_STEP = 8
ACT_REDUCE_UNROLL = 16
SLOTS = PEER_HEADS * PEER_TOPK
LANES = 128
HALF_D = D_MODEL // 2
ROWS_PER_EXPERT = HALF_D // LANES
TABLE_PAD_ROWS = 8 * ROWS_PER_EXPERT
SUBLANES = 8


def _take_top(s, ids, n):
    vals, picks = [], []
    for _ in range(n):
        m = jnp.max(s, axis=0, keepdims=True)
        sel = jnp.min(jnp.where(s == m, ids, jnp.int32(2 ** 30)), axis=0, keepdims=True)
        s = jnp.where(ids == sel, -jnp.inf, s)
        vals.append(m)
        picks.append(sel)
    return vals, picks


def _peer_select_kernel(x_ref, g_ref, wq_ref, keys_ref, h_ref, idx_ref, gate_ref, q_scr):
    step = pl.program_id(1)
    t = x_ref.shape[0]

    @pl.when(step == 0)
    def _():
        h = _rms(x_ref[...], g_ref[...])
        h_ref[...] = h
        q = jnp.dot(h.astype(jnp.bfloat16), wq_ref[...], preferred_element_type=jnp.float32)
        for k in range(PEER_HEADS):
            q_scr[k] = q[:, k * 2 * PEER_HALF:(k + 1) * 2 * PEER_HALF].astype(jnp.bfloat16)

    for j in range(SELECT_HEADS_PER_STEP):
        q = q_scr[step * SELECT_HEADS_PER_STEP + j]
        gate, idx = _select_head(q, keys_ref[j, 0], keys_ref[j, 1], t)
        gate_ref[0, j * PEER_TOPK:(j + 1) * PEER_TOPK, :] = gate
        idx_ref[j] = idx


def _select_head(q, keys1, keys2, t):
    key_id = lax.broadcasted_iota(jnp.int32, (N_KEYS, t), 0)
    halves = []
    for p, keys in enumerate((keys1, keys2)):
        s = lax.dot_general(keys, q[:, p * PEER_HALF:(p + 1) * PEER_HALF],
                            (((1,), (1,)), ((), ())), preferred_element_type=jnp.float32)
        vals, ids = _take_top(s, key_id, PEER_TOPK)
        halves.append((jnp.concatenate(vals, axis=0), jnp.concatenate(ids, axis=0)))
    (s1, i1), (s2, i2) = halves

    n_experts = N_KEYS * N_KEYS
    cv, cid = [], []
    for b in range(SUBLANES):
        rows = PEER_TOPK if b == 0 else SUBLANES
        a_id = lax.broadcasted_iota(jnp.int32, (rows, t), 0)
        v = s1[0:rows] + s2[b:b + 1]
        na = PEER_TOPK // (b + 1)
        if na < rows:
            v = jnp.where(a_id < na, v, -jnp.inf)
        cv.append(v)
        cid.append((a_id * PEER_TOPK + b) * n_experts + i1[0:rows] * N_KEYS + i2[b:b + 1])
    b_id = lax.broadcasted_iota(jnp.int32, (SUBLANES, t), 0) + SUBLANES
    cv.append(s1[0:1] + s2[SUBLANES:])
    cid.append(b_id * n_experts + i1[0:1] * N_KEYS + i2[SUBLANES:])
    top, picks = _take_top(jnp.concatenate(cv, axis=0), jnp.concatenate(cid, axis=0), PEER_TOPK)
    top = jnp.concatenate(top, axis=0)
    e = jnp.exp(top - top[0:1])
    experts = jnp.concatenate(picks, axis=0) & (n_experts - 1)
    return e / jnp.sum(e, axis=0, keepdims=True), experts * ROWS_PER_EXPERT + TABLE_PAD_ROWS


def _peer_select(x, g, wq_bf16, keys_bf16):
    n, d = x.shape
    t = PEER_SELECT_TILE
    nt = n // t
    hps = SELECT_HEADS_PER_STEP
    return pl.pallas_call(
        _peer_select_kernel,
        grid=(nt, PEER_HEADS // hps),
        in_specs=[
            pl.BlockSpec((t, d), lambda i, h: (i, 0)),
            pl.BlockSpec((1, d), lambda i, h: (0, 0)),
            pl.BlockSpec(wq_bf16.shape, lambda i, h: (0, 0)),
            pl.BlockSpec((hps, 2, N_KEYS, PEER_HALF), lambda i, h: (h, 0, 0, 0)),
        ],
        out_specs=[pl.BlockSpec((t, d), lambda i, h: (i, 0)),
                   pl.BlockSpec((hps, PEER_TOPK, t), lambda i, h: (h, i, 0)),
                   pl.BlockSpec((1, hps * PEER_TOPK, t), lambda i, h: (i, h, 0))],
        out_shape=[jax.ShapeDtypeStruct((n, d), jnp.float32),
                   jax.ShapeDtypeStruct((PEER_HEADS, nt * PEER_TOPK, t), jnp.int32),
                   jax.ShapeDtypeStruct((nt, SLOTS, t), jnp.float32)],
        scratch_shapes=[pltpu.VMEM((PEER_HEADS, t, 2 * PEER_HALF), jnp.bfloat16)],
        compiler_params=_params(2, V7X_VMEM_LIMIT_BYTES),
        name="peer_select",
    )(x, g.reshape(1, d), wq_bf16, keys_bf16)


def _pack_table(tab):
    zero_experts = jnp.zeros((TABLE_PAD_ROWS // ROWS_PER_EXPERT, tab.shape[1]), tab.dtype)
    tab = jnp.concatenate([zero_experts, tab, zero_experts], axis=0)
    bits = lax.bitcast_convert_type(tab.astype(jnp.bfloat16), jnp.uint16).astype(jnp.uint32)
    packed = bits[:, :HALF_D] | (bits[:, HALF_D:] << 16)
    return packed.reshape(tab.shape[0] * ROWS_PER_EXPERT, LANES)


def _load_expert_pair(tab_ref, head_idx_ref, k, col, low_sublanes):
    ra = pl.multiple_of(head_idx_ref[k * PEER_SELECT_TILE + col], ROWS_PER_EXPERT)
    rb = pl.multiple_of(head_idx_ref[(k + 1) * PEER_SELECT_TILE + col] - ROWS_PER_EXPERT, ROWS_PER_EXPERT)
    pair = jnp.where(low_sublanes, tab_ref[pl.ds(ra, SUBLANES), :], tab_ref[pl.ds(rb, SUBLANES), :])
    lo = pltpu.bitcast(pair << 16, jnp.float32)
    hi = pltpu.bitcast(pair & jnp.uint32(0xFFFF0000), jnp.float32)
    return lo, hi


def _expert_pairs(tab_ref, idx_refs, t, low_sublanes):
    per_select = PEER_SELECT_TILE // PEER_TILE
    col = (pl.program_id(0) % per_select) * PEER_TILE + t
    for kk in range(PEER_TOPK // 2):
        for hd in range(PEER_HEADS):
            lo, hi = _load_expert_pair(tab_ref, idx_refs[hd], 2 * kk, col, low_sublanes)
            yield hd * (PEER_TOPK // 2) + kk, lo, hi


def _to_token_rows(src_ref, rows_scr):
    t = src_ref.shape[0]
    for c in range(SUBLANES):
        rows_scr[pl.ds(c, t, stride=SUBLANES), :] = src_ref[:, c * LANES:(c + 1) * LANES]


def _peer_act_kernel(*refs):
    idx_refs = refs[:PEER_HEADS]
    h_ref, tab_ref, gate_ref, w_ref, hrow_scr, act_scr = refs[PEER_HEADS:PEER_HEADS + 6]
    prod_scrs = refs[PEER_HEADS + 6:]
    group = len(prod_scrs)
    t_tile = PEER_TILE
    lane = lax.broadcasted_iota(jnp.int32, (SLOTS, t_tile), 1)
    low = lax.broadcasted_iota(jnp.int32, (SUBLANES, LANES), 0) < ROWS_PER_EXPERT
    _to_token_rows(h_ref, hrow_scr)
    for prod_scr in prod_scrs:
        prod_scr[...] = jnp.zeros(prod_scr.shape, jnp.float32)
    act_scr[...] = jnp.zeros(act_scr.shape, jnp.float32)

    def products(t, prod_scr):
        hv = hrow_scr[pl.ds(pl.multiple_of(t * SUBLANES, SUBLANES), SUBLANES), :]
        swapped = pltpu.roll(hv, ROWS_PER_EXPERT, 0)
        h_lo = jnp.where(low, hv, swapped)
        h_hi = jnp.where(low, swapped, hv)
        for p, lo, hi in _expert_pairs(tab_ref, idx_refs, t, low):
            prod_scr[p * SUBLANES:(p + 1) * SUBLANES, :] = lo * h_lo + hi * h_hi

    def dots(prod_scr):
        part = prod_scr[pl.ds(0, SLOTS, stride=ROWS_PER_EXPERT), :]
        for k in range(1, ROWS_PER_EXPERT):
            part = part + prod_scr[pl.ds(k, SLOTS, stride=ROWS_PER_EXPERT), :]
        return jnp.sum(part, axis=1, keepdims=True)

    def with_dots(act, t0):
        for k, prod_scr in enumerate(prod_scrs):
            act = jnp.where(lane == t0 + k, dots(prod_scr), act)
        return act

    def token_group(i, carry):
        t0 = group * i
        act_scr[...] = with_dots(act_scr[...], t0 - group)
        for k, prod_scr in enumerate(prod_scrs):
            products(t0 + k, prod_scr)
        return carry

    lax.fori_loop(0, t_tile // group, token_group, 0)
    act = with_dots(act_scr[...], t_tile - group)
    gelu = 0.5 * act * (1.0 + lax.erf(act * (0.5 ** 0.5)))
    w_ref[0] = gate_ref[0] * gelu


def _head_idx_specs():
    per_select = PEER_SELECT_TILE // PEER_TILE
    return [pl.BlockSpec((PEER_TOPK * PEER_SELECT_TILE,), lambda i: (i // per_select,),
                         memory_space=pltpu.SMEM)] * PEER_HEADS


def _peer_act(head_idx, h, u_packed, gate):
    t = PEER_TILE
    n, d = h.shape
    nt = n // t
    per_gate = PEER_SELECT_TILE // PEER_TILE
    prod = pltpu.VMEM((SLOTS * ROWS_PER_EXPERT, LANES), jnp.float32)
    return pl.pallas_call(
        _peer_act_kernel,
        grid=(nt,),
        in_specs=_head_idx_specs() + [
            pl.BlockSpec((t, d), lambda i: (i, 0)),
            pl.BlockSpec(u_packed.shape, lambda i: (0, 0), pipeline_mode=pl.Buffered(1)),
            pl.BlockSpec((1, SLOTS, t), lambda i: (i // per_gate, 0, i % per_gate)),
        ],
        out_specs=pl.BlockSpec((1, SLOTS, t), lambda i: (i, 0, 0)),
        out_shape=jax.ShapeDtypeStruct((nt, SLOTS, t), jnp.float32),
        scratch_shapes=[pltpu.VMEM((t * SUBLANES, LANES), jnp.float32),
                        pltpu.VMEM((SLOTS, t), jnp.float32)] + [prod] * ACT_TOKENS_PER_STEP,
        compiler_params=_params(1, V7X_VMEM_LIMIT_BYTES),
        name="peer_act",
    )(*head_idx, h, u_packed, gate)


def _peer_out_kernel(final_norm, *refs):
    idx_refs = refs[:PEER_HEADS]
    w_ref, w_next_ref, x_ref, tab_ref, g_ref, o_ref, xrow_scr, wb_even, wb_odd = refs[PEER_HEADS:]
    step = pl.program_id(0)
    t_tile = PEER_TILE
    lane = lax.broadcasted_iota(jnp.int32, (SLOTS, t_tile), 1)
    low = lax.broadcasted_iota(jnp.int32, (SUBLANES, LANES), 0) < ROWS_PER_EXPERT
    _to_token_rows(x_ref, xrow_scr)

    def weight_rows(t, tile_w_ref):
        w_col = jnp.sum(jnp.where(lane == t, tile_w_ref[0], 0.0), axis=1, keepdims=True)
        return jnp.broadcast_to(w_col, (SLOTS, LANES))

    def accumulate(t, wb_ref):
        acc_lo = jnp.zeros((SUBLANES, LANES), jnp.float32)
        acc_hi = jnp.zeros((SUBLANES, LANES), jnp.float32)
        for p, lo, hi in _expert_pairs(tab_ref, idx_refs, t, low):
            w2 = jnp.where(low, jnp.broadcast_to(wb_ref[2 * p:2 * p + 1, :], (SUBLANES, LANES)),
                           jnp.broadcast_to(wb_ref[2 * p + 1:2 * p + 2, :], (SUBLANES, LANES)))
            acc_lo = acc_lo + w2 * lo
            acc_hi = acc_hi + w2 * hi
        y = jnp.where(low, acc_lo + pltpu.roll(acc_lo, ROWS_PER_EXPERT, 0),
                      acc_hi + pltpu.roll(acc_hi, ROWS_PER_EXPERT, 0))
        rows = pl.ds(pl.multiple_of(t * SUBLANES, SUBLANES), SUBLANES)
        xrow_scr[rows, :] = xrow_scr[rows, :] + y

    @pl.when(step == 0)
    def _():
        def spread_weights(i, carry):
            for k in range(ACT_REDUCE_UNROLL):
                t = i * ACT_REDUCE_UNROLL + k
                wb_even[t] = weight_rows(t, w_ref)
            return carry

        lax.fori_loop(0, t_tile // ACT_REDUCE_UNROLL, spread_weights, 0)

    def run(wb_cur, wb_next):
        def token_group(i, carry):
            for k in range(OUT_TOKENS_PER_STEP):
                t = i * OUT_TOKENS_PER_STEP + k
                accumulate(t, wb_cur.at[t])
                wb_next[t] = weight_rows(t, w_next_ref)
            return carry

        lax.fori_loop(0, t_tile // OUT_TOKENS_PER_STEP, token_group, 0)

    @pl.when(step % 2 == 0)
    def _():
        run(wb_even, wb_odd)

    @pl.when(step % 2 == 1)
    def _():
        run(wb_odd, wb_even)

    for c in range(SUBLANES):
        o_ref[:, c * LANES:(c + 1) * LANES] = xrow_scr[pl.ds(c, t_tile, stride=SUBLANES), :]
    if final_norm:
        o_ref[...] = _rms(o_ref[...], g_ref[...])


def _peer_out(head_idx, w, x, v_packed, g_final):
    t = PEER_TILE
    n, d = x.shape
    nt = n // t
    g = jnp.ones((d,), jnp.float32) if g_final is None else g_final
    wb = pltpu.VMEM((t, SLOTS, LANES), jnp.float32)
    return pl.pallas_call(
        functools.partial(_peer_out_kernel, g_final is not None),
        grid=(nt,),
        in_specs=_head_idx_specs() + [
            pl.BlockSpec((1, SLOTS, t), lambda i: (i, 0, 0)),
            pl.BlockSpec((1, SLOTS, t), lambda i: (jnp.minimum(i + 1, nt - 1), 0, 0)),
            pl.BlockSpec((t, d), lambda i: (i, 0)),
            pl.BlockSpec(v_packed.shape, lambda i: (0, 0), pipeline_mode=pl.Buffered(1)),
            pl.BlockSpec((1, d), lambda i: (0, 0)),
        ],
        out_specs=pl.BlockSpec((t, d), lambda i: (i, 0)),
        out_shape=jax.ShapeDtypeStruct((n, d), jnp.float32),
        scratch_shapes=[pltpu.VMEM((t * SUBLANES, LANES), jnp.float32), wb, wb],
        compiler_params=_params(1, V7X_VMEM_LIMIT_BYTES),
        name="peer_out",
    )(*head_idx, w, w, x, v_packed, g.reshape(1, d))


def peer_layer(x, g, w_query, sub_keys, u_packed, v_packed, g_final=None):
    h, idx, gate = _peer_select(x, g, w_query.astype(jnp.bfloat16), sub_keys.astype(jnp.bfloat16))
    flat = idx.reshape(PEER_HEADS, -1)
    head_idx = [flat[hd] for hd in range(PEER_HEADS)]
    w = _peer_act(head_idx, h, u_packed, gate)
    return _peer_out(head_idx, w, x, v_packed, g_final)


def _trunk(x, seq_len, l0, l1, g_final):
    b = x.shape[0]
    x = x.reshape(b * seq_len, D_MODEL)
    x = conv_mixer_layer(x, l0["norm_mix"], l0["w_in"], l0["conv_w"], l0["w_out"], seq_len)
    x = peer_layer(x, l0["norm_ffn"], l0["w_query"], l0["sub_keys"], l0["u"], l0["v"])
    x = attn_mixer_layer(x, l1["norm_mix"], l1["w_qkv"], l1["w_o"], seq_len)
    x = peer_layer(x, l1["norm_ffn"], l1["w_query"], l1["sub_keys"], l1["u"], l1["v"], g_final)
    return x.reshape(b, seq_len, D_MODEL)


def kernel(x_prompt, x_sample, l0_norm_mix, l0_conv_w_in, l0_conv_w, l0_conv_w_out, l0_norm_ffn, l0_peer_w_query, l0_peer_sub_keys, l0_peer_u, l0_peer_v, l1_norm_mix, l1_attn_w_qkv, l1_attn_w_o, l1_norm_ffn, l1_peer_w_query, l1_peer_sub_keys, l1_peer_u, l1_peer_v, norm_final):
    l0 = dict(norm_mix=l0_norm_mix, w_in=l0_conv_w_in, conv_w=l0_conv_w, w_out=l0_conv_w_out,
              norm_ffn=l0_norm_ffn, w_query=l0_peer_w_query, sub_keys=l0_peer_sub_keys,
              u=_pack_table(l0_peer_u), v=_pack_table(l0_peer_v))
    l1 = dict(norm_mix=l1_norm_mix, w_qkv=l1_attn_w_qkv, w_o=l1_attn_w_o,
              norm_ffn=l1_norm_ffn, w_query=l1_peer_w_query, sub_keys=l1_peer_sub_keys,
              u=_pack_table(l1_peer_u), v=_pack_table(l1_peer_v))
    return (_trunk(x_prompt, x_prompt.shape[1], l0, l1, norm_final),
            _trunk(x_sample, x_sample.shape[1], l0, l1, norm_final))
```

```python
import functools

import jax
import jax.numpy as jnp
from jax import lax
from jax.experimental import pallas as pl
from jax.experimental.pallas import tpu as pltpu

D_MODEL = 1024
RMS_EPS = 1e-6
TOKEN_TILE = 512
V7X_VMEM_LIMIT_BYTES = 56 * 1024 * 1024


def _params(n_grid_dims, vmem=None):
    return pltpu.CompilerParams(
        dimension_semantics=("arbitrary",) * n_grid_dims,
        vmem_limit_bytes=vmem)


def _rms(x, g):
    return x * lax.rsqrt(jnp.mean(x * x, axis=-1, keepdims=True) + RMS_EPS) * g


def _conv_in_kernel(x_ref, g_ref, w_ref, b_ref, z_ref):
    h = _rms(x_ref[...], g_ref[...]).astype(jnp.bfloat16)
    y = jnp.dot(h, w_ref[...], preferred_element_type=jnp.float32)
    d = D_MODEL
    b_ref[...] = y[:, :d]
    z_ref[...] = y[:, d:2 * d] * y[:, 2 * d:]


def _conv_in(x, g, w_in_bf16):
    n, d = x.shape
    tm = TOKEN_TILE
    return pl.pallas_call(
        _conv_in_kernel,
        grid=(n // tm,),
        in_specs=[
            pl.BlockSpec((tm, d), lambda i: (i, 0)),
            pl.BlockSpec((1, d), lambda i: (0, 0)),
            pl.BlockSpec((d, 3 * d), lambda i: (0, 0)),
        ],
        out_specs=[pl.BlockSpec((tm, d), lambda i: (i, 0)),
                   pl.BlockSpec((tm, d), lambda i: (i, 0))],
        out_shape=[jax.ShapeDtypeStruct((n, d), jnp.float32)] * 2,
        compiler_params=_params(1, V7X_VMEM_LIMIT_BYTES),
        name="conv_in",
    )(x, g.reshape(1, d), w_in_bf16)


def _conv_out_kernel(tiles_per_seq, x_ref, b_ref, z_ref, zp_ref, zn_ref, cw_ref, w_ref, o_ref):
    i = pl.program_id(0)
    pos = i % tiles_per_seq
    z = z_ref[...]
    tm = z.shape[0]
    row = lax.broadcasted_iota(jnp.int32, z.shape, 0)
    prev_row = jnp.where(pos == 0, 0.0, zp_ref[7:8, :])
    next_row = jnp.where(pos == tiles_per_seq - 1, 0.0, zn_ref[0:1, :])
    z_prev = jnp.where(row == 0, prev_row, pltpu.roll(z, 1, 0))
    z_next = jnp.where(row == tm - 1, next_row, pltpu.roll(z, tm - 1, 0))
    cw = cw_ref[...]
    conv = z_prev * cw[0:1, :] + z * cw[1:2, :] + z_next * cw[2:3, :]
    a = (b_ref[...] * conv).astype(jnp.bfloat16)
    o_ref[...] = x_ref[...] + jnp.dot(a, w_ref[...], preferred_element_type=jnp.float32)


def _conv_out(x, b, z, conv_w, w_out_bf16, seq_len):
    n, d = x.shape
    tm = TOKEN_TILE
    tps = seq_len // tm
    sub = tm // 8
    nb8 = n // 8
    tile = lambda i: (i, 0)
    return pl.pallas_call(
        functools.partial(_conv_out_kernel, tps),
        grid=(n // tm,),
        in_specs=[
            pl.BlockSpec((tm, d), tile),
            pl.BlockSpec((tm, d), tile),
            pl.BlockSpec((tm, d), tile),
            pl.BlockSpec((8, d), lambda i: (jnp.maximum(i * sub - 1, 0), 0)),
            pl.BlockSpec((8, d), lambda i: (jnp.minimum((i + 1) * sub, nb8 - 1), 0)),
            pl.BlockSpec((3, d), lambda i: (0, 0)),
            pl.BlockSpec((d, d), lambda i: (0, 0)),
        ],
        out_specs=pl.BlockSpec((tm, d), tile),
        out_shape=jax.ShapeDtypeStruct((n, d), jnp.float32),
        compiler_params=_params(1, V7X_VMEM_LIMIT_BYTES),
        name="conv_out",
    )(x, b, z, z, z, conv_w, w_out_bf16)


def conv_mixer_layer(x, g, w_in, conv_w, w_out, seq_len):
    b, z = _conv_in(x, g, w_in.astype(jnp.bfloat16))
    return _conv_out(x, b, z, conv_w, w_out.astype(jnp.bfloat16), seq_len)


DILATED_CONFIGS = ((128, 1), (512, 4), (2048, 16))
N_GROUPS = len(DILATED_CONFIGS)
HEADS_PER_GROUP = 16
HEAD_DIM = D_MODEL // HEADS_PER_GROUP
ATTN_SIDE = 64
ATTN_Q_BLOCK = 128
LSE_LANES = 128


def _qkv_kernel(x_ref, g_ref, w_ref, *refs):
    out_refs = refs[:N_GROUPS]
    h_scr, hp_scr = refs[N_GROUPS:]
    h = _rms(x_ref[...], g_ref[...])
    d = D_MODEL
    tm = x_ref.shape[0]
    for k in range(d // LANES):
        h_scr[k] = h[:, k * LANES:(k + 1) * LANES]
    for group, (_, dil) in enumerate(DILATED_CONFIGS):
        rows = tm // dil
        if dil == 1:
            hp = h.astype(jnp.bfloat16)
        else:
            for r in range(dil):
                for k in range(d // LANES):
                    hp_scr[r * rows:(r + 1) * rows, k * LANES:(k + 1) * LANES] = (
                        h_scr[k, pl.ds(r, rows, stride=dil), :].astype(jnp.bfloat16))
            hp = hp_scr[...]
        for c in range(3):
            col = (group * 3 + c) * d
            y = jnp.dot(hp, w_ref[:, col:col + d], preferred_element_type=jnp.float32).astype(jnp.bfloat16)
            for r in range(dil):
                out_refs[group][0, :, (r * 3 + c) * d:(r * 3 + c + 1) * d] = y[r * rows:(r + 1) * rows]


def _qkv_proj(x, g, w_bf16, seq_len):
    n, d = x.shape
    tm = TOKEN_TILE
    tps = seq_len // tm
    batch = n // seq_len
    return pl.pallas_call(
        _qkv_kernel,
        grid=(n // tm,),
        in_specs=[
            pl.BlockSpec((tm, d), lambda i: (i, 0)),
            pl.BlockSpec((1, d), lambda i: (0, 0)),
            pl.BlockSpec(w_bf16.shape, lambda i: (0, 0), pipeline_mode=pl.Buffered(1)),
        ],
        out_specs=[pl.BlockSpec((1, tm // dil, dil * 3 * d), lambda i: (i // tps, i % tps, 0))
                   for _, dil in DILATED_CONFIGS],
        out_shape=[jax.ShapeDtypeStruct((batch, seq_len // dil, dil * 3 * d), jnp.bfloat16)
                   for _, dil in DILATED_CONFIGS],
        scratch_shapes=[pltpu.VMEM((d // LANES, tm, LANES), jnp.float32),
                        pltpu.VMEM((tm, d), jnp.bfloat16)],
        compiler_params=_params(1, V7X_VMEM_LIMIT_BYTES),
        name="qkv_proj",
    )(x, g.reshape(1, d), w_bf16)


def _alibi_slope(head):
    n = N_GROUPS * HEADS_PER_GROUP
    return 2.0 ** (-8.0 * (head + 1) / n)


def _attn_kernel(group, dilation, seq_strided, q_ref, kp_ref, kc_ref, kn_ref,
                 vp_ref, vc_ref, vn_ref, o_ref, lse_ref, k_scr, v_scr):
    j = pl.program_id(2)
    qb = ATTN_Q_BLOCK
    nk = qb + 2 * ATTN_SIDE
    for scr, (p_ref, c_ref, n_ref) in ((k_scr, (kp_ref, kc_ref, kn_ref)), (v_scr, (vp_ref, vc_ref, vn_ref))):
        scr[0:ATTN_SIDE] = p_ref[0]
        scr[ATTN_SIDE:ATTN_SIDE + qb] = c_ref[0]
        scr[ATTN_SIDE + qb:nk] = n_ref[0]
    qi = lax.broadcasted_iota(jnp.int32, (qb, nk), 0)
    ci = lax.broadcasted_iota(jnp.int32, (qb, nk), 1)
    rel = ci - ATTN_SIDE - qi
    key_pos = j * qb - ATTN_SIDE + ci
    valid = (jnp.abs(rel) <= ATTN_SIDE) & (key_pos >= 0) & (key_pos < seq_strided)
    dist = jnp.where(valid, (dilation * jnp.abs(rel)).astype(jnp.float32), jnp.inf)
    lane = lax.broadcasted_iota(jnp.int32, (qb, LSE_LANES), 1)
    first = lane < HEAD_DIM
    pair_mask = [first.astype(jnp.bfloat16), (~first).astype(jnp.bfloat16)]
    lse_all = jnp.zeros((qb, LSE_LANES), jnp.float32)
    for pair in range(HEADS_PER_GROUP // 2):
        cols = slice(pair * LANES, (pair + 1) * LANES)
        q2, k2, v2 = q_ref[0, :, cols], k_scr[:, cols], v_scr[:, cols]
        outs = []
        for half in range(2):
            h = 2 * pair + half
            s = lax.dot_general(q2 * pair_mask[half], k2, (((1,), (1,)), ((), ())),
                                preferred_element_type=jnp.float32)
            slope = _alibi_slope(group * HEADS_PER_GROUP + h)
            s = s * (HEAD_DIM ** -0.5) - slope * dist
            m = jnp.max(s, axis=-1, keepdims=True)
            p = jnp.exp(s - m)
            den = jnp.sum(p, axis=-1, keepdims=True)
            o = jnp.dot(p.astype(jnp.bfloat16), v2, preferred_element_type=jnp.float32)
            outs.append(o / den)
            lse_all = jnp.where(lane == h, m + jnp.log(den), lse_all)
        o_ref[0, :, cols] = jnp.where(first, outs[0], outs[1])
    lse_ref[0] = lse_all


def _group_attention(view, group, dilation):
    d = D_MODEL
    batch, ls = view.shape[0], view.shape[1]
    width = 3
    qb = ATTN_Q_BLOCK
    sb = qb // ATTN_SIDE
    n_side = ls // ATTN_SIDE
    col0 = 0

    def cur(c):
        return pl.BlockSpec((1, qb, d), lambda b, r, j: (b, j, r * width + col0 + c))

    def prev(c):
        return pl.BlockSpec((1, ATTN_SIDE, d),
                            lambda b, r, j: (b, jnp.maximum(j * sb - 1, 0), r * width + col0 + c))

    def nxt(c):
        return pl.BlockSpec((1, ATTN_SIDE, d),
                            lambda b, r, j: (b, jnp.minimum((j + 1) * sb, n_side - 1), r * width + col0 + c))

    return pl.pallas_call(
        functools.partial(_attn_kernel, group, dilation, ls),
        grid=(batch, dilation, ls // qb),
        in_specs=[cur(0), prev(1), cur(1), nxt(1), prev(2), cur(2), nxt(2)],
        out_specs=[pl.BlockSpec((1, qb, d), lambda b, r, j: (b, j, r)),
                   pl.BlockSpec((1, qb, LSE_LANES), lambda b, r, j: (b, j, r))],
        out_shape=[jax.ShapeDtypeStruct((batch, ls, dilation * d), jnp.float32),
                   jax.ShapeDtypeStruct((batch, ls, dilation * LSE_LANES), jnp.float32)],
        scratch_shapes=[pltpu.VMEM((qb + 2 * ATTN_SIDE, d), jnp.bfloat16)] * 2,
        compiler_params=_params(3, V7X_VMEM_LIMIT_BYTES),
        name=f"attn_g{group}",
    )(view, view, view, view, view, view, view)


def _token_order(ref, width, dil, scr):
    if dil == 1:
        return ref[0]
    rows = ref.shape[1]
    chunks = width // LANES
    for r in range(dil):
        for k in range(chunks):
            col = r * width + k * LANES
            scr[k, pl.ds(r, rows, stride=dil), :] = ref[0, :, col:col + LANES]
    return jnp.concatenate([scr[k] for k in range(chunks)], axis=1)


def _merge_kernel(x_ref, o0_ref, o1_ref, o2_ref, l0_ref, l1_ref, l2_ref, e_ref, w_ref, out_ref,
                  o_scr, l_scr):
    dils = [dil for _, dil in DILATED_CONFIGS]
    l0, l1, l2 = [_token_order(ref, LSE_LANES, dil, l_scr)
                  for ref, dil in zip((l0_ref, l1_ref, l2_ref), dils)]
    m = jnp.maximum(jnp.maximum(l0, l1), l2)
    e0, e1, e2 = jnp.exp(l0 - m), jnp.exp(l1 - m), jnp.exp(l2 - m)
    tot = e0 + e1 + e2
    def expand(a):
        hi = a.astype(jnp.bfloat16)
        rest = a - hi.astype(jnp.float32)
        mid = rest.astype(jnp.bfloat16)
        lo = (rest - mid.astype(jnp.float32)).astype(jnp.bfloat16)
        return jnp.dot(jnp.concatenate([hi, mid, lo], axis=1), e_ref[...],
                       preferred_element_type=jnp.float32)
    merged = jnp.zeros(x_ref.shape, jnp.float32)
    for e, o_ref, dil in zip((e0, e1, e2), (o0_ref, o1_ref, o2_ref), dils):
        merged = merged + expand(e / tot) * _token_order(o_ref, D_MODEL, dil, o_scr)
    out_ref[...] = x_ref[...] + jnp.dot(merged.astype(jnp.bfloat16), w_ref[...],
                                        preferred_element_type=jnp.float32)


def _attn_merge(x, outs, lses, w_o_bf16, seq_len):
    n, d = x.shape
    tm = TOKEN_TILE
    tps = seq_len // tm
    head_of_col = jnp.arange(d, dtype=jnp.int32) // HEAD_DIM
    expand = (jnp.arange(LSE_LANES, dtype=jnp.int32)[:, None] == head_of_col[None, :]).astype(jnp.bfloat16)
    expand = jnp.concatenate([expand] * 3, axis=0)
    tile = lambda i: (i, 0)
    const = lambda i: (0, 0)
    strided = lambda i: (i // tps, i % tps, 0)
    dils = [dil for _, dil in DILATED_CONFIGS]
    return pl.pallas_call(
        _merge_kernel,
        grid=(n // tm,),
        in_specs=[pl.BlockSpec((tm, d), tile)]
        + [pl.BlockSpec((1, tm // dil, dil * d), strided) for dil in dils]
        + [pl.BlockSpec((1, tm // dil, dil * LSE_LANES), strided) for dil in dils]
        + [pl.BlockSpec((3 * LSE_LANES, d), const), pl.BlockSpec((d, d), const)],
        out_specs=pl.BlockSpec((tm, d), tile),
        out_shape=jax.ShapeDtypeStruct((n, d), jnp.float32),
        scratch_shapes=[pltpu.VMEM((d // LANES, tm, LANES), jnp.float32),
                        pltpu.VMEM((1, tm, LSE_LANES), jnp.float32)],
        compiler_params=_params(1, V7X_VMEM_LIMIT_BYTES),
        name="attn_merge",
    )(x, *outs, *lses, expand, w_o_bf16)


def attn_mixer_layer(x, g, w_qkv, w_o, seq_len):
    views = _qkv_proj(x, g, w_qkv.astype(jnp.bfloat16), seq_len)
    outs, lses = [], []
    for group, (window, dilation) in enumerate(DILATED_CONFIGS):
        assert window // (2 * dilation) == ATTN_SIDE
        o, lse = _group_attention(views[group], group, dilation)
        outs.append(o)
        lses.append(lse)
    return _attn_merge(x, outs, lses, w_o.astype(jnp.bfloat16), seq_len)


PEER_HEADS = 8
PEER_HALF = 128
N_KEYS = 128
PEER_TOPK = 16
PEER_TILE = 128
PEER_SELECT_TILE = 512
SELECT_HEADS_PER_STEP = 2
OUT_TOKENS_PER_STEP = 8
ACT_TOKENS_PER_STEP = 8
ACT_REDUCE_UNROLL = 16
SLOTS = PEER_HEADS * PEER_TOPK
N_IDX_REFS = 2 * PEER_HEADS
LANES = 128
HALF_D = D_MODEL // 2
ROWS_PER_EXPERT = HALF_D // LANES
TABLE_PAD_ROWS = 8 * ROWS_PER_EXPERT
SUBLANES = 8


def _take_top(s, ids, n):
    vals, picks = [], []
    for _ in range(n):
        m = jnp.max(s, axis=0, keepdims=True)
        sel = jnp.min(jnp.where(s == m, ids, jnp.int32(2 ** 30)), axis=0, keepdims=True)
        s = jnp.where(ids == sel, -jnp.inf, s)
        vals.append(m)
        picks.append(sel)
    return vals, picks


def _peer_select_kernel(x_ref, g_ref, wq_ref, keys_ref, h_ref, idx_ref, gate_ref, q_scr):
    step = pl.program_id(1)
    t = x_ref.shape[0]

    @pl.when(step == 0)
    def _():
        h = _rms(x_ref[...], g_ref[...])
        h_ref[...] = h
        q = jnp.dot(h.astype(jnp.bfloat16), wq_ref[...], preferred_element_type=jnp.float32)
        for k in range(PEER_HEADS):
            q_scr[k] = q[:, k * 2 * PEER_HALF:(k + 1) * 2 * PEER_HALF].astype(jnp.bfloat16)

    for j in range(SELECT_HEADS_PER_STEP):
        q = q_scr[step * SELECT_HEADS_PER_STEP + j]
        gate, idx = _select_head(q, keys_ref[j, 0], keys_ref[j, 1], t)
        gate_ref[0, j * PEER_TOPK:(j + 1) * PEER_TOPK, :] = gate
        idx_ref[j] = idx


def _select_head(q, keys1, keys2, t):
    key_id = lax.broadcasted_iota(jnp.int32, (N_KEYS, t), 0)
    halves = []
    for p, keys in enumerate((keys1, keys2)):
        s = lax.dot_general(keys, q[:, p * PEER_HALF:(p + 1) * PEER_HALF],
                            (((1,), (1,)), ((), ())), preferred_element_type=jnp.float32)
        vals, ids = _take_top(s, key_id, PEER_TOPK)
        halves.append((jnp.concatenate(vals, axis=0), jnp.concatenate(ids, axis=0)))
    (s1, i1), (s2, i2) = halves

    n_experts = N_KEYS * N_KEYS
    cv, cid = [], []
    for b in range(SUBLANES):
        rows = PEER_TOPK if b == 0 else SUBLANES
        a_id = lax.broadcasted_iota(jnp.int32, (rows, t), 0)
        v = s1[0:rows] + s2[b:b + 1]
        na = PEER_TOPK // (b + 1)
        if na < rows:
            v = jnp.where(a_id < na, v, -jnp.inf)
        cv.append(v)
        cid.append((a_id * PEER_TOPK + b) * n_experts + i1[0:rows] * N_KEYS + i2[b:b + 1])
    b_id = lax.broadcasted_iota(jnp.int32, (SUBLANES, t), 0) + SUBLANES
    cv.append(s1[0:1] + s2[SUBLANES:])
    cid.append(b_id * n_experts + i1[0:1] * N_KEYS + i2[SUBLANES:])
    top, picks = _take_top(jnp.concatenate(cv, axis=0), jnp.concatenate(cid, axis=0), PEER_TOPK)
    top = jnp.concatenate(top, axis=0)
    e = jnp.exp(top - top[0:1])
    experts = jnp.concatenate(picks[0::2] + picks[1::2], axis=0) & (n_experts - 1)
    return e / jnp.sum(e, axis=0, keepdims=True), experts * ROWS_PER_EXPERT + TABLE_PAD_ROWS


def _peer_select(x, g, wq_bf16, keys_bf16):
    n, d = x.shape
    t = PEER_SELECT_TILE
    nt = n // t
    hps = SELECT_HEADS_PER_STEP
    return pl.pallas_call(
        _peer_select_kernel,
        grid=(nt, PEER_HEADS // hps),
        in_specs=[
            pl.BlockSpec((t, d), lambda i, h: (i, 0)),
            pl.BlockSpec((1, d), lambda i, h: (0, 0)),
            pl.BlockSpec(wq_bf16.shape, lambda i, h: (0, 0)),
            pl.BlockSpec((hps, 2, N_KEYS, PEER_HALF), lambda i, h: (h, 0, 0, 0)),
        ],
        out_specs=[pl.BlockSpec((t, d), lambda i, h: (i, 0)),
                   pl.BlockSpec((hps, PEER_TOPK, t), lambda i, h: (h, i, 0)),
                   pl.BlockSpec((1, hps * PEER_TOPK, t), lambda i, h: (i, h, 0))],
        out_shape=[jax.ShapeDtypeStruct((n, d), jnp.float32),
                   jax.ShapeDtypeStruct((PEER_HEADS, nt * PEER_TOPK, t), jnp.int32),
                   jax.ShapeDtypeStruct((nt, SLOTS, t), jnp.float32)],
        scratch_shapes=[pltpu.VMEM((PEER_HEADS, t, 2 * PEER_HALF), jnp.bfloat16)],
        compiler_params=_params(2, V7X_VMEM_LIMIT_BYTES),
        name="peer_select",
    )(x, g.reshape(1, d), wq_bf16, keys_bf16)


def _pack_table(tab):
    zero_experts = jnp.zeros((TABLE_PAD_ROWS // ROWS_PER_EXPERT, tab.shape[1]), tab.dtype)
    tab = jnp.concatenate([zero_experts, tab, zero_experts], axis=0)
    bits = lax.bitcast_convert_type(tab.astype(jnp.bfloat16), jnp.uint16).astype(jnp.uint32)
    packed = bits[:, :HALF_D] | (bits[:, HALF_D:] << 16)
    return packed.reshape(tab.shape[0] * ROWS_PER_EXPERT, LANES)


def _load_expert_pair(tab_ref, even_ref, odd_ref, even_offset, odd_offset, low_sublanes):
    ra = pl.multiple_of(even_ref[even_offset], ROWS_PER_EXPERT)
    rb = pl.multiple_of(odd_ref[odd_offset] - ROWS_PER_EXPERT, ROWS_PER_EXPERT)
    pair = jnp.where(low_sublanes, tab_ref[pl.ds(ra, SUBLANES), :], tab_ref[pl.ds(rb, SUBLANES), :])
    lo = pltpu.bitcast(pair << 16, jnp.float32)
    hi = pltpu.bitcast(pair & jnp.uint32(0xFFFF0000), jnp.float32)
    return lo, hi


def _expert_pairs(tab_ref, idx_refs, t, low_sublanes):
    per_select = PEER_SELECT_TILE // PEER_TILE
    col = (pl.program_id(0) % per_select) * PEER_TILE + t
    split = len(idx_refs) == 2 * PEER_HEADS
    odd_rows = 0 if split else (PEER_TOPK // 2) * PEER_SELECT_TILE
    for kk in range(PEER_TOPK // 2):
        offset = kk * PEER_SELECT_TILE + col
        for hd in range(PEER_HEADS):
            even_ref = idx_refs[2 * hd] if split else idx_refs[hd]
            odd_ref = idx_refs[2 * hd + 1] if split else idx_refs[hd]
            lo, hi = _load_expert_pair(tab_ref, even_ref, odd_ref, offset, offset + odd_rows, low_sublanes)
            yield hd * (PEER_TOPK // 2) + kk, lo, hi


def _to_token_rows(src_ref, rows_scr):
    t = src_ref.shape[0]
    for c in range(SUBLANES):
        rows_scr[pl.ds(c, t, stride=SUBLANES), :] = src_ref[:, c * LANES:(c + 1) * LANES]


def _peer_act_kernel(*refs):
    idx_refs = refs[:N_IDX_REFS]
    h_ref, tab_ref, gate_ref, w_ref, hrow_scr, act_scr = refs[N_IDX_REFS:N_IDX_REFS + 6]
    prod_scrs = refs[N_IDX_REFS + 6:]
    group = len(prod_scrs)
    t_tile = PEER_TILE
    lane = lax.broadcasted_iota(jnp.int32, (SLOTS, t_tile), 1)
    low = lax.broadcasted_iota(jnp.int32, (SUBLANES, LANES), 0) < ROWS_PER_EXPERT
    _to_token_rows(h_ref, hrow_scr)
    for prod_scr in prod_scrs:
        prod_scr[...] = jnp.zeros(prod_scr.shape, jnp.float32)
    act_scr[...] = jnp.zeros(act_scr.shape, jnp.float32)

    def products(t, prod_scr):
        hv = hrow_scr[pl.ds(pl.multiple_of(t * SUBLANES, SUBLANES), SUBLANES), :]
        swapped = pltpu.roll(hv, ROWS_PER_EXPERT, 0)
        h_lo = jnp.where(low, hv, swapped)
        h_hi = jnp.where(low, swapped, hv)
        for p, lo, hi in _expert_pairs(tab_ref, idx_refs, t, low):
            prod_scr[p * SUBLANES:(p + 1) * SUBLANES, :] = lo * h_lo + hi * h_hi

    def dots(prod_scr):
        part = prod_scr[pl.ds(0, SLOTS, stride=ROWS_PER_EXPERT), :]
        for k in range(1, ROWS_PER_EXPERT):
            part = part + prod_scr[pl.ds(k, SLOTS, stride=ROWS_PER_EXPERT), :]
        return jnp.sum(part, axis=1, keepdims=True)

    def with_dots(act, t0):
        for k, prod_scr in enumerate(prod_scrs):
            act = jnp.where(lane == t0 + k, dots(prod_scr), act)
        return act

    def token_group(i, carry):
        t0 = group * i
        act_scr[...] = with_dots(act_scr[...], t0 - group)
        for k, prod_scr in enumerate(prod_scrs):
            products(t0 + k, prod_scr)
        return carry

    lax.fori_loop(0, t_tile // group, token_group, 0)
    act = with_dots(act_scr[...], t_tile - group)
    gelu = 0.5 * act * (1.0 + lax.erf(act * (0.5 ** 0.5)))
    w_ref[0] = gate_ref[0] * gelu


def _head_idx_specs(split):
    per_select = PEER_SELECT_TILE // PEER_TILE
    if not split:
        return [pl.BlockSpec((PEER_TOPK * PEER_SELECT_TILE,), lambda i: (i // per_select,),
                             memory_space=pltpu.SMEM)] * PEER_HEADS
    half = (PEER_TOPK // 2) * PEER_SELECT_TILE
    even = pl.BlockSpec((half,), lambda i: (2 * (i // per_select),), memory_space=pltpu.SMEM)
    odd = pl.BlockSpec((half,), lambda i: (2 * (i // per_select) + 1,), memory_space=pltpu.SMEM)
    return [even, odd] * PEER_HEADS


def _peer_act(head_idx, h, u_packed, gate):
    t = PEER_TILE
    n, d = h.shape
    nt = n // t
    per_gate = PEER_SELECT_TILE // PEER_TILE
    prod = pltpu.VMEM((SLOTS * ROWS_PER_EXPERT, LANES), jnp.float32)
    return pl.pallas_call(
        _peer_act_kernel,
        grid=(nt,),
        in_specs=_head_idx_specs(split=True) + [
            pl.BlockSpec((t, d), lambda i: (i, 0)),
            pl.BlockSpec(u_packed.shape, lambda i: (0, 0), pipeline_mode=pl.Buffered(1)),
            pl.BlockSpec((1, SLOTS, t), lambda i: (i // per_gate, 0, i % per_gate)),
        ],
        out_specs=pl.BlockSpec((1, SLOTS, t), lambda i: (i, 0, 0)),
        out_shape=jax.ShapeDtypeStruct((nt, SLOTS, t), jnp.float32),
        scratch_shapes=[pltpu.VMEM((t * SUBLANES, LANES), jnp.float32),
                        pltpu.VMEM((SLOTS, t), jnp.float32)] + [prod] * ACT_TOKENS_PER_STEP,
        compiler_params=_params(1, V7X_VMEM_LIMIT_BYTES),
        name="peer_act",
    )(*head_idx, h, u_packed, gate)


def _peer_out_kernel(final_norm, *refs):
    idx_refs = refs[:PEER_HEADS]
    w_ref, w_next_ref, x_ref, tab_ref, g_ref, o_ref, xrow_scr, wb_even, wb_odd = refs[PEER_HEADS:]
    step = pl.program_id(0)
    t_tile = PEER_TILE
    lane = lax.broadcasted_iota(jnp.int32, (SLOTS, t_tile), 1)
    low = lax.broadcasted_iota(jnp.int32, (SUBLANES, LANES), 0) < ROWS_PER_EXPERT
    _to_token_rows(x_ref, xrow_scr)

    def weight_rows(t, tile_w_ref):
        w_col = jnp.sum(jnp.where(lane == t, tile_w_ref[0], 0.0), axis=1, keepdims=True)
        return jnp.broadcast_to(w_col, (SLOTS, LANES))

    def accumulate(t, wb_ref):
        acc_lo = jnp.zeros((SUBLANES, LANES), jnp.float32)
        acc_hi = jnp.zeros((SUBLANES, LANES), jnp.float32)
        for p, lo, hi in _expert_pairs(tab_ref, idx_refs, t, low):
            w2 = jnp.where(low, jnp.broadcast_to(wb_ref[2 * p:2 * p + 1, :], (SUBLANES, LANES)),
                           jnp.broadcast_to(wb_ref[2 * p + 1:2 * p + 2, :], (SUBLANES, LANES)))
            acc_lo = acc_lo + w2 * lo
            acc_hi = acc_hi + w2 * hi
        y = jnp.where(low, acc_lo + pltpu.roll(acc_lo, ROWS_PER_EXPERT, 0),
                      acc_hi + pltpu.roll(acc_hi, ROWS_PER_EXPERT, 0))
        rows = pl.ds(pl.multiple_of(t * SUBLANES, SUBLANES), SUBLANES)
        xrow_scr[rows, :] = xrow_scr[rows, :] + y

    @pl.when(step == 0)
    def _():
        def spread_weights(i, carry):
            for k in range(ACT_REDUCE_UNROLL):
                t = i * ACT_REDUCE_UNROLL + k
                wb_even[t] = weight_rows(t, w_ref)
            return carry

        lax.fori_loop(0, t_tile // ACT_REDUCE_UNROLL, spread_weights, 0)

    def run(wb_cur, wb_next):
        def token_group(i, carry):
            for k in range(OUT_TOKENS_PER_STEP):
                t = i * OUT_TOKENS_PER_STEP + k
                accumulate(t, wb_cur.at[t])
                wb_next[t] = weight_rows(t, w_next_ref)
            return carry

        lax.fori_loop(0, t_tile // OUT_TOKENS_PER_STEP, token_group, 0)

    @pl.when(step % 2 == 0)
    def _():
        run(wb_even, wb_odd)

    @pl.when(step % 2 == 1)
    def _():
        run(wb_odd, wb_even)

    for c in range(SUBLANES):
        o_ref[:, c * LANES:(c + 1) * LANES] = xrow_scr[pl.ds(c, t_tile, stride=SUBLANES), :]
    if final_norm:
        o_ref[...] = _rms(o_ref[...], g_ref[...])


def _peer_out(head_idx, w, x, v_packed, g_final):
    t = PEER_TILE
    n, d = x.shape
    nt = n // t
    g = jnp.ones((d,), jnp.float32) if g_final is None else g_final
    wb = pltpu.VMEM((t, SLOTS, LANES), jnp.float32)
    return pl.pallas_call(
        functools.partial(_peer_out_kernel, g_final is not None),
        grid=(nt,),
        in_specs=_head_idx_specs(split=False) + [
            pl.BlockSpec((1, SLOTS, t), lambda i: (i, 0, 0)),
            pl.BlockSpec((1, SLOTS, t), lambda i: (jnp.minimum(i + 1, nt - 1), 0, 0)),
            pl.BlockSpec((t, d), lambda i: (i, 0)),
            pl.BlockSpec(v_packed.shape, lambda i: (0, 0), pipeline_mode=pl.Buffered(1)),
            pl.BlockSpec((1, d), lambda i: (0, 0)),
        ],
        out_specs=pl.BlockSpec((t, d), lambda i: (i, 0)),
        out_shape=jax.ShapeDtypeStruct((n, d), jnp.float32),
        scratch_shapes=[pltpu.VMEM((t * SUBLANES, LANES), jnp.float32), wb, wb],
        compiler_params=_params(1, V7X_VMEM_LIMIT_BYTES),
        name="peer_out",
    )(*head_idx[::2], w, w, x, v_packed, g.reshape(1, d))


def peer_layer(x, g, w_query, sub_keys, u_packed, v_packed, g_final=None):
    h, idx, gate = _peer_select(x, g, w_query.astype(jnp.bfloat16), sub_keys.astype(jnp.bfloat16))
    flat = idx.reshape(PEER_HEADS, -1)
    head_idx = [flat[hd] for hd in range(PEER_HEADS) for _ in range(2)]
    w = _peer_act(head_idx, h, u_packed, gate)
    return _peer_out(head_idx, w, x, v_packed, g_final)


def _trunk(x, seq_len, l0, l1, g_final):
    b = x.shape[0]
    x = x.reshape(b * seq_len, D_MODEL)
    x = conv_mixer_layer(x, l0["norm_mix"], l0["w_in"], l0["conv_w"], l0["w_out"], seq_len)
    x = peer_layer(x, l0["norm_ffn"], l0["w_query"], l0["sub_keys"], l0["u"], l0["v"])
    x = attn_mixer_layer(x, l1["norm_mix"], l1["w_qkv"], l1["w_o"], seq_len)
    x = peer_layer(x, l1["norm_ffn"], l1["w_query"], l1["sub_keys"], l1["u"], l1["v"], g_final)
    return x.reshape(b, seq_len, D_MODEL)


def kernel(x_prompt, x_sample, l0_norm_mix, l0_conv_w_in, l0_conv_w, l0_conv_w_out, l0_norm_ffn, l0_peer_w_query, l0_peer_sub_keys, l0_peer_u, l0_peer_v, l1_norm_mix, l1_attn_w_qkv, l1_attn_w_o, l1_norm_ffn, l1_peer_w_query, l1_peer_sub_keys, l1_peer_u, l1_peer_v, norm_final):
    l0 = dict(norm_mix=l0_norm_mix, w_in=l0_conv_w_in, conv_w=l0_conv_w, w_out=l0_conv_w_out,
              norm_ffn=l0_norm_ffn, w_query=l0_peer_w_query, sub_keys=l0_peer_sub_keys,
              u=_pack_table(l0_peer_u), v=_pack_table(l0_peer_v))
    l1 = dict(norm_mix=l1_norm_mix, w_qkv=l1_attn_w_qkv, w_o=l1_attn_w_o,
              norm_ffn=l1_norm_ffn, w_query=l1_peer_w_query, sub_keys=l1_peer_sub_keys,
              u=_pack_table(l1_peer_u), v=_pack_table(l1_peer_v))
    return (_trunk(x_prompt, x_prompt.shape[1], l0, l1, norm_final),
            _trunk(x_sample, x_sample.shape[1], l0, l1, norm_final))
```
